```python
import math
import jax, jax.numpy as jnp
from jax import lax
import numpy as np

D_MODEL = 2048
BATCH = 2
SEQ = 8192
DEPTH = 1

N_META = 16
GRID_W = 64
DA_HEADS = 8
DA_HEAD_DIM = 64
DA_V_DIM = 2 * DA_HEAD_DIM
DA_Q_BLOCK = 128
DA_QK_W = DA_HEADS * 2 * DA_HEAD_DIM
DA_V_W = DA_HEADS * DA_V_DIM
NA_HEADS = 16
NA_HEAD_DIM = 64
NA_W = NA_HEADS * NA_HEAD_DIM
NA_WIN_H = 8
NA_WIN_W = 16
IN_SPLITS = (DA_QK_W, DA_QK_W, DA_V_W, NA_W, NA_W, NA_W, D_MODEL, D_MODEL)
IN_COLS = sum(IN_SPLITS)
PEER_HEADS = 8
PEER_N_KEYS = 128
PEER_N_EXPERTS = PEER_N_KEYS * PEER_N_KEYS
PEER_KEY_DIM = 256
PEER_TOPK = 16
PEER_CHUNK = 128
EPS = 1e-6

kernel_name = "hybrid_diffattn_natten_peer_block"


def rms_norm(x, g):
    xf = x.astype(jnp.float32)
    y = xf * lax.rsqrt(jnp.mean(xf * xf, axis=-1, keepdims=True) + EPS)
    return (y * g.astype(jnp.float32)).astype(x.dtype)


def alibi_slopes(n_heads):
    return jnp.asarray(2.0 ** (-8.0 * np.arange(1, n_heads + 1) / n_heads), dtype=jnp.float32)


def diff_attention(q, k, v, lam, slopes):
    B, L, H = q.shape[0], q.shape[1], q.shape[2]
    scale = DA_HEAD_DIM ** -0.5
    kpos = jnp.arange(L, dtype=jnp.float32)

    def attend(q_blk, qpos):
        s = jnp.einsum('bqhcd,bkhcd->bchqk', q_blk, k).astype(jnp.float32) * scale
        bias = -slopes[:, None, None] * jnp.abs(qpos[:, None] - kpos[None, :])[None]
        p = jax.nn.softmax(s + bias, axis=-1)
        a = p[:, 0] - lam * p[:, 1]
        return jnp.einsum('bhqk,bkhe->bqhe', a.astype(v.dtype), v)

    meta_out = attend(q[:, :N_META], jnp.arange(N_META, dtype=jnp.float32))
    n_real = L - N_META
    nb = n_real // DA_Q_BLOCK
    q_real = q[:, N_META:].reshape(B, nb, DA_Q_BLOCK, H, 2, DA_HEAD_DIM).transpose(1, 0, 2, 3, 4, 5)
    qpos_real = (N_META + jnp.arange(n_real, dtype=jnp.float32)).reshape(nb, DA_Q_BLOCK)
    real_out = lax.map(lambda args: attend(args[0], args[1]), (q_real, qpos_real))
    real_out = real_out.transpose(1, 0, 2, 3, 4).reshape(B, n_real, H, DA_V_DIM)
    return jnp.concatenate([meta_out, real_out], axis=1)


def neighbourhood_attention(q, k, v, rpb):
    B, L, H, dh = q.shape
    n_real = L - N_META
    rows = n_real // GRID_W
    kh = min(NA_WIN_H, rows)
    kw = NA_WIN_W
    scale = dh ** -0.5
    qm, km, vm = q[:, :N_META], k[:, :N_META], v[:, :N_META]
    sm = jnp.einsum('bqhd,bmhd->bhqm', qm, km).astype(jnp.float32) * scale
    meta_out = jnp.einsum('bhqm,bmhd->bqhd', jax.nn.softmax(sm, axis=-1).astype(v.dtype), vm)

    qg = q[:, N_META:].reshape(B, rows, GRID_W, H, dh)
    kg = k[:, N_META:].reshape(B, rows, GRID_W, H, dh)
    vg = v[:, N_META:].reshape(B, rows, GRID_W, H, dh)
    cols = jnp.arange(GRID_W)
    col_start = jnp.clip(cols - kw // 2, 0, GRID_W - kw)
    col_idx = col_start[:, None] + jnp.arange(kw)[None, :]
    dc_idx = col_idx - cols[:, None] + (NA_WIN_W - 1)

    def attend_row(args):
        q_row, r = args
        r0 = jnp.clip(r - kh // 2, 0, rows - kh)
        k_rows = lax.dynamic_slice_in_dim(kg, r0, kh, axis=1)
        v_rows = lax.dynamic_slice_in_dim(vg, r0, kh, axis=1)
        k_nb = k_rows[:, :, col_idx]
        v_nb = v_rows[:, :, col_idx]
        s_win = jnp.einsum('bqhd,bjqwhd->bhqjw', q_row, k_nb).astype(jnp.float32) * scale
        dr_idx = r0 + jnp.arange(kh) - r + (NA_WIN_H - 1)
        bias = rpb[:, dr_idx[None, :, None], dc_idx[:, None, :]]
        s_win = s_win + bias.astype(jnp.float32)[None]
        s_meta = jnp.einsum('bqhd,bmhd->bhqm', q_row, km).astype(jnp.float32) * scale
        s = jnp.concatenate([s_win.reshape(B, H, GRID_W, kh * kw), s_meta], axis=-1)
        p = jax.nn.softmax(s, axis=-1).astype(v.dtype)
        p_win = p[..., :kh * kw].reshape(B, H, GRID_W, kh, kw)
        p_meta = p[..., kh * kw:]
        return (jnp.einsum('bhqjw,bjqwhd->bqhd', p_win, v_nb)
                + jnp.einsum('bhqm,bmhd->bqhd', p_meta, vm))

    out = lax.map(attend_row, (qg.transpose(1, 0, 2, 3, 4), jnp.arange(rows)))
    out = out.transpose(1, 0, 2, 3, 4).reshape(B, n_real, H, dh)
    return jnp.concatenate([meta_out, out], axis=1)


def peer_ffn(x, w_query, sub_keys, expert_u, expert_v):
    B, L, D = x.shape
    T = B * L
    n_chunks = -(-T // PEER_CHUNK)
    pad = n_chunks * PEER_CHUNK - T
    xt = jnp.pad(x.reshape(T, D), ((0, pad), (0, 0))).reshape(n_chunks, PEER_CHUNK, D)

    def chunk(xc):
        q = (xc @ w_query).reshape(PEER_CHUNK, PEER_HEADS, 2, PEER_KEY_DIM // 2)
        s = jnp.einsum('chpk,pnk->chpn', q, sub_keys).astype(jnp.float32)
        s1, i1 = lax.top_k(s[:, :, 0], PEER_TOPK)
        s2, i2 = lax.top_k(s[:, :, 1], PEER_TOPK)
        cand = (s1[..., :, None] + s2[..., None, :]).reshape(PEER_CHUNK, PEER_HEADS, PEER_TOPK * PEER_TOPK)
        sc, ci = lax.top_k(cand, PEER_TOPK)
        e = (jnp.take_along_axis(i1, ci // PEER_TOPK, axis=-1) * PEER_N_KEYS
             + jnp.take_along_axis(i2, ci % PEER_TOPK, axis=-1))
        g = jax.nn.softmax(sc, axis=-1)
        u = expert_u[e]
        v = expert_v[e]
        a = jax.nn.gelu(jnp.einsum('cd,chkd->chk', xc, u).astype(jnp.float32))
        return jnp.einsum('chk,chkd->cd', (g * a).astype(xc.dtype), v)

    out = lax.map(chunk, xt).reshape(n_chunks * PEER_CHUNK, D)[:T]
    return out.reshape(B, L, D)


def setup_inputs(seed: int = 0) -> dict:
    key = jax.random.key(seed)
    ks = jax.random.split(key, 24)
    f32 = jnp.float32
    nrm = lambda k, shape, s: jax.random.normal(k, shape, f32) * s
    gain = lambda k, shape: 1.0 + 0.01 * jax.random.normal(k, shape, f32)
    return {
        'x': nrm(ks[0], (BATCH, SEQ, D_MODEL), 1.0),
        'meta_tokens': nrm(ks[1], (N_META, D_MODEL), 1.0),
        'norm1_g': gain(ks[2], (DEPTH, D_MODEL)),
        'w_in': nrm(ks[3], (DEPTH, D_MODEL, IN_COLS), D_MODEL ** -0.5),
        'da_q_norm_g': gain(ks[4], (DEPTH, DA_HEAD_DIM)),
        'da_k_norm_g': gain(ks[5], (DEPTH, DA_HEAD_DIM)),
        'lambda_q1': nrm(ks[6], (DEPTH, DA_HEAD_DIM), 0.1),
        'lambda_k1': nrm(ks[7], (DEPTH, DA_HEAD_DIM), 0.1),
        'lambda_q2': nrm(ks[8], (DEPTH, DA_HEAD_DIM), 0.1),
        'lambda_k2': nrm(ks[9], (DEPTH, DA_HEAD_DIM), 0.1),
        'da_out_norm_g': gain(ks[10], (DEPTH, DA_V_DIM)),
        'na_q_norm_g': gain(ks[11], (DEPTH, NA_HEAD_DIM)),
        'na_k_norm_g': gain(ks[12], (DEPTH, NA_HEAD_DIM)),
        'na_rpb': nrm(ks[13], (DEPTH, NA_HEADS, 2 * NA_WIN_H - 1, 2 * NA_WIN_W - 1), 0.1),
        'w_branch_da': nrm(ks[14], (DEPTH, DA_V_W, D_MODEL), DA_V_W ** -0.5),
        'w_branch_na': nrm(ks[15], (DEPTH, NA_W, D_MODEL), NA_W ** -0.5),
        'w_out': nrm(ks[16], (DEPTH, D_MODEL, D_MODEL), D_MODEL ** -0.5),
        'norm2_g': gain(ks[17], (DEPTH, D_MODEL)),
        'peer_w_query': nrm(ks[18], (DEPTH, D_MODEL, PEER_HEADS * PEER_KEY_DIM), D_MODEL ** -0.5),
        'peer_sub_keys': nrm(ks[19], (DEPTH, 2, PEER_N_KEYS, PEER_KEY_DIM // 2), (PEER_KEY_DIM // 2) ** -0.5),
        'peer_u': nrm(ks[20], (DEPTH, PEER_N_EXPERTS, D_MODEL), D_MODEL ** -0.5),
        'peer_v': nrm(ks[21], (DEPTH, PEER_N_EXPERTS, D_MODEL), 0.5),
    }


def reference(x, meta_tokens, norm1_g, w_in, da_q_norm_g, da_k_norm_g, lambda_q1, lambda_k1,
              lambda_q2, lambda_k2, da_out_norm_g, na_q_norm_g, na_k_norm_g, na_rpb,
              w_branch_da, w_branch_na, w_out, norm2_g, peer_w_query, peer_sub_keys, peer_u, peer_v):
    B, S, D = x.shape
    L = S + N_META
    meta = jnp.broadcast_to(meta_tokens[None].astype(x.dtype), (B, N_META, D))
    h = jnp.concatenate([meta, x], axis=1)
    slopes = alibi_slopes(DA_HEADS)
    split_pts = [int(c) for c in np.cumsum(IN_SPLITS)[:-1]]
    for l in range(DEPTH):
        xn = rms_norm(h, norm1_g[l])
        proj = xn @ w_in[l]
        q_da, k_da, v_da, q_na, k_na, v_na, gate_da, gate_na = jnp.split(proj, split_pts, axis=-1)

        q_da = rms_norm(q_da.reshape(B, L, DA_HEADS, 2, DA_HEAD_DIM), da_q_norm_g[l])
        k_da = rms_norm(k_da.reshape(B, L, DA_HEADS, 2, DA_HEAD_DIM), da_k_norm_g[l])
        v_da = v_da.reshape(B, L, DA_HEADS, DA_V_DIM)
        lam_init = 0.8 - 0.6 * math.exp(-0.3 * l)
        lam = (jnp.exp(jnp.sum(lambda_q1[l].astype(jnp.float32) * lambda_k1[l].astype(jnp.float32)))
               - jnp.exp(jnp.sum(lambda_q2[l].astype(jnp.float32) * lambda_k2[l].astype(jnp.float32)))
               + lam_init)
        o_da = diff_attention(q_da, k_da, v_da, lam, slopes)
        o_da = rms_norm(o_da, da_out_norm_g[l]) * (1.0 - lam_init)
        y_da = o_da.reshape(B, L, DA_V_W) @ w_branch_da[l]

        q_na = rms_norm(q_na.reshape(B, L, NA_HEADS, NA_HEAD_DIM), na_q_norm_g[l])
        k_na = rms_norm(k_na.reshape(B, L, NA_HEADS, NA_HEAD_DIM), na_k_norm_g[l])
        v_na = v_na.reshape(B, L, NA_HEADS, NA_HEAD_DIM)
        o_na = neighbourhood_attention(q_na, k_na, v_na, na_rpb[l])
        y_na = o_na.reshape(B, L, NA_W) @ w_branch_na[l]

        mixed = jax.nn.sigmoid(gate_da) * y_da + jax.nn.sigmoid(gate_na) * y_na
        h = h + mixed @ w_out[l]

        h = h + peer_ffn(rms_norm(h, norm2_g[l]), peer_w_query[l], peer_sub_keys[l], peer_u[l], peer_v[l])
    return h[:, N_META:]
```

```python
import functools
import math

import numpy as np
import jax
import jax.numpy as jnp
from jax import lax
from jax.experimental import pallas as pl
from jax.experimental.pallas import tpu as pltpu

F32 = jnp.float32
BF16 = jnp.bfloat16

GRID_W = 64
NA_WIN_H = 8
NA_WIN_W = 16
PEER_TOPK = 16
EPS = 1e-6
NEG = -1e30

LANES = 128
MXU_DIM = 256
VMEM_LIMIT = 56 * 1024 * 1024

ROW_BLOCK = 256

_NT = (((1,), (1,)), ((), ()))


def _cparams(sem):
    return pltpu.CompilerParams(dimension_semantics=sem, vmem_limit_bytes=VMEM_LIMIT)


def _inproj_body(x_ref, g1_ref, w_ref, gc_ref, o_ref, xn_ref, *, norm_ranges, sig_start, head_dim):
    j = pl.program_id(1)

    @pl.when(j == 0)
    def _():
        x = x_ref[...]
        ms = jnp.mean(x * x, axis=-1, keepdims=True)
        xn_ref[...] = (x * lax.rsqrt(ms + EPS) * g1_ref[...]).astype(BF16)

    y = jnp.dot(xn_ref[...], w_ref[...], preferred_element_type=F32)
    tn = y.shape[1]

    is_norm = functools.reduce(jnp.logical_or, [(j >= a) & (j < b) for a, b in norm_ranges])
    is_sig = j >= sig_start

    @pl.when(is_norm)
    def _():
        r = lax.broadcasted_iota(jnp.int32, (MXU_DIM, MXU_DIM), 0) // head_dim
        c = lax.broadcasted_iota(jnp.int32, (MXU_DIM, MXU_DIM), 1) // head_dim
        ones_bd = (r == c).astype(BF16)
        yy = (y * y).astype(BF16)
        ss = jnp.concatenate(
            [jnp.dot(yy[:, t * MXU_DIM:(t + 1) * MXU_DIM], ones_bd, preferred_element_type=F32)
             for t in range(tn // MXU_DIM)], axis=1)
        o_ref[...] = (y * lax.rsqrt(ss * (1.0 / head_dim) + EPS) * gc_ref[...]).astype(o_ref.dtype)

    @pl.when(is_sig)
    def _():
        o_ref[...] = jax.nn.sigmoid(y).astype(o_ref.dtype)

    @pl.when(jnp.logical_not(is_norm | is_sig))
    def _():
        o_ref[...] = y.astype(o_ref.dtype)


def _inproj(h0, g1, w_bf, gcols, *, norm_ranges, sig_start, head_dim, tm, tn):
    T, D = h0.shape
    C = w_bf.shape[1]
    body = functools.partial(_inproj_body, norm_ranges=norm_ranges, sig_start=sig_start,
                             head_dim=head_dim)
    return pl.pallas_call(
        body,
        grid=(T // tm, C // tn),
        in_specs=[
            pl.BlockSpec((tm, D), lambda i, j: (i, 0)),
            pl.BlockSpec((1, D), lambda i, j: (0, 0)),
            pl.BlockSpec((D, tn), lambda i, j: (0, j)),
            pl.BlockSpec((1, tn), lambda i, j: (0, j)),
        ],
        out_specs=pl.BlockSpec((tm, tn), lambda i, j: (i, j)),
        out_shape=jax.ShapeDtypeStruct((T, C), BF16),
        scratch_shapes=[pltpu.VMEM((tm, D), BF16)],
        compiler_params=_cparams(("parallel", "arbitrary")),
        name="inproj",
    )(h0, g1, w_bf, gcols)


def _da_body(slopes_ref, q_ref, k_ref, v_ref, lamv_ref, og_ref, o_ref, *,
             S, n_meta, tq, tk, tkm, hd, lam_init):
    h = pl.program_id(1)
    i = pl.program_id(2)
    slope = slopes_ref[h]

    q = q_ref[...]
    lane = lax.broadcasted_iota(jnp.int32, q.shape, 1)
    zero = jnp.zeros_like(q)
    qs = (jnp.where(lane < hd, q, zero), jnp.where(lane >= hd, q, zero))
    qidx = i * tq + lax.broadcasted_iota(jnp.int32, (tq, 1), 0)
    qpos = jnp.where(qidx < S, qidx + n_meta, qidx - S).astype(F32)

    def step(start, size, carry):
        k = k_ref[pl.ds(start, size), :]
        v = v_ref[pl.ds(start, size), :]
        kidx = start + lax.broadcasted_iota(jnp.int32, (1, size), 1)
        kpos = jnp.where(kidx < S, kidx + n_meta, kidx - S).astype(F32)
        bias = -slope * jnp.abs(qpos - kpos)
        bias = jnp.where(kidx < S + n_meta, bias, NEG)
        out = []
        for c in range(2):
            m, l, acc = carry[c]
            s = lax.dot_general(qs[c], k, _NT, preferred_element_type=F32) + bias
            m_new = jnp.maximum(m, jnp.max(s, axis=-1, keepdims=True))
            alpha = jnp.exp(m - m_new)
            p = jnp.exp(s - m_new)
            l = alpha * l + jnp.sum(p, axis=-1, keepdims=True)
            acc = alpha * acc + jnp.dot(p.astype(BF16), v, preferred_element_type=F32)
            out.append((m_new, l, acc))
        return tuple(out)

    dv = v_ref.shape[1]
    init = tuple((jnp.full((tq, 1), NEG, F32), jnp.zeros((tq, 1), F32), jnp.zeros((tq, dv), F32))
                 for _ in range(2))
    carry = lax.fori_loop(0, S // tk,
                          lambda jb, cr: step(pl.multiple_of(jb * tk, tk), tk, cr), init)
    carry = step(S, tkm, carry)

    lv = lamv_ref[...]
    lam = (jnp.exp(jnp.sum(lv[0:1] * lv[1:2], axis=-1, keepdims=True))
           - jnp.exp(jnp.sum(lv[2:3] * lv[3:4], axis=-1, keepdims=True)) + lam_init)
    (_, l0, a0), (_, l1, a1) = carry
    o = a0 / l0 - lam * (a1 / l1)
    o = o * lax.rsqrt(jnp.mean(o * o, axis=-1, keepdims=True) + EPS) * og_ref[...] * (1.0 - lam_init)
    o_ref[...] = o.astype(o_ref.dtype)


def _diff_attention(proj, slopes, lamv, og, *, B, Lp, S, n_meta, H, hd, q_col, k_col, v_col,
                    lam_init, tq, tk, tkm):
    dv = 2 * hd
    nq = Lp // tq
    body = functools.partial(_da_body, S=S, n_meta=n_meta, tq=tq, tk=tk, tkm=tkm, hd=hd,
                             lam_init=lam_init)
    qb, kb, vb = q_col // dv, k_col // dv, v_col // dv
    return pl.pallas_call(
        body,
        grid=(B, H, nq),
        in_specs=[
            pl.BlockSpec(memory_space=pltpu.SMEM),
            pl.BlockSpec((tq, dv), lambda b, h, i: (b * nq + i, qb + h)),
            pl.BlockSpec((Lp, dv), lambda b, h, i: (b, kb + h)),
            pl.BlockSpec((Lp, dv), lambda b, h, i: (b, vb + h)),
            pl.BlockSpec(lamv.shape, lambda b, h, i: (0, 0)),
            pl.BlockSpec((1, dv), lambda b, h, i: (0, 0)),
        ],
        out_specs=pl.BlockSpec((tq, dv), lambda b, h, i: (b * nq + i, h)),
        out_shape=jax.ShapeDtypeStruct((B * Lp, H * dv), BF16),
        compiler_params=_cparams(("parallel", "parallel", "arbitrary")),
        name="diff_attention",
    )(slopes, proj, proj, proj, lamv, og)


def _na_bias_table(rpb, rows, qrows):
    W = GRID_W
    Q = qrows * W
    n_real = rows // qrows
    kh = min(NA_WIN_H, rows)
    idx_dr, idx_dc, valid = [], [], []
    for qb, kb0 in ((0, 0), (1, 0), (n_real - 1, n_real - 3)):
        ql = np.arange(Q)
        kl = np.arange(3 * Q)
        r = (qb * qrows + ql // W)[:, None]
        qc = (ql % W)[:, None]
        kr = (kb0 * qrows + kl // W)[None, :]
        kc = (kl % W)[None, :]
        r0 = np.clip(r - kh // 2, 0, rows - kh)
        cs = np.clip(qc - NA_WIN_W // 2, 0, W - NA_WIN_W)
        ok = (kr >= r0) & (kr < r0 + kh) & (kc >= cs) & (kc < cs + NA_WIN_W)
        dr = np.clip(kr - r + (NA_WIN_H - 1), 0, 2 * NA_WIN_H - 2)
        dc = np.clip(kc - qc + (NA_WIN_W - 1), 0, 2 * NA_WIN_W - 2)
        idx_dr.append(np.broadcast_to(dr, ok.shape))
        idx_dc.append(np.broadcast_to(dc, ok.shape))
        valid.append(ok)
    idx_dr, idx_dc, valid = np.stack(idx_dr), np.stack(idx_dc), np.stack(valid)
    flat = rpb.reshape(rpb.shape[0], -1).astype(F32)
    g = jnp.take(flat, jnp.asarray(idx_dr * rpb.shape[2] + idx_dc), axis=1)
    g = jnp.where(jnp.asarray(valid)[None], g, NEG)
    g = jnp.concatenate([g, jnp.full_like(g[:, :1], NEG)], axis=1)
    return g.transpose(1, 0, 2, 3)


def _na_body(q_ref, k0_ref, k1_ref, k2_ref, v0_ref, v1_ref, v2_ref, km_ref, vm_ref, bias_ref,
             o_ref, *, hd, n_meta):
    q = q_ref[...]
    k = jnp.concatenate([k0_ref[...], k1_ref[...], k2_ref[...]], axis=0)
    v = jnp.concatenate([v0_ref[...], v1_ref[...], v2_ref[...]], axis=0)
    km = km_ref[...]
    vm = vm_ref[...]
    lane = lax.broadcasted_iota(jnp.int32, q.shape, 1)
    zero = jnp.zeros_like(q)
    midx = lax.broadcasted_iota(jnp.int32, (1, km.shape[0]), 1)
    mmask = jnp.where(midx < n_meta, 0.0, NEG)
    outs = []
    for hh in range(2):
        qh = jnp.where((lane >= hh * hd) & (lane < (hh + 1) * hd), q, zero)
        s = lax.dot_general(qh, k, _NT, preferred_element_type=F32) + bias_ref[0, hh]
        sm = lax.dot_general(qh, km, _NT, preferred_element_type=F32) + mmask
        m = jnp.maximum(jnp.max(s, axis=-1, keepdims=True), jnp.max(sm, axis=-1, keepdims=True))
        p = jnp.exp(s - m)
        pm = jnp.exp(sm - m)
        l = jnp.sum(p, axis=-1, keepdims=True) + jnp.sum(pm, axis=-1, keepdims=True)
        o = (jnp.dot(p.astype(BF16), v, preferred_element_type=F32)
             + jnp.dot(pm.astype(BF16), vm, preferred_element_type=F32))
        outs.append(o / l)
    o_ref[...] = jnp.where(lane < hd, outs[0], outs[1]).astype(o_ref.dtype)


def _nbhd_attention(proj, bias_tab, *, B, Lp, S, n_meta, H, hd, q_col, k_col, v_col, qrows):
    Q = qrows * GRID_W
    nqb = Lp // Q
    n_real = S // Q
    w2 = 2 * hd
    qc, kc, vc = q_col // w2, k_col // w2, v_col // w2
    mrows = LANES

    def kb0(i):
        return jnp.clip(i - 1, 0, n_real - 3)

    def variant(i):
        return jnp.where(i == 0, 0, jnp.where(i == n_real - 1, 2, jnp.where(i >= n_real, 3, 1)))

    def kv_spec(col, d):
        return pl.BlockSpec((Q, w2), lambda hp, b, i: (b * nqb + kb0(i) + d, col + hp))

    def meta_spec(col):
        return pl.BlockSpec((mrows, w2), lambda hp, b, i: ((b * Lp + S) // mrows, col + hp))

    body = functools.partial(_na_body, hd=hd, n_meta=n_meta)
    return pl.pallas_call(
        body,
        grid=(H // 2, B, nqb),
        in_specs=[pl.BlockSpec((Q, w2), lambda hp, b, i: (b * nqb + i, qc + hp)),
                  kv_spec(kc, 0), kv_spec(kc, 1), kv_spec(kc, 2),
                  kv_spec(vc, 0), kv_spec(vc, 1), kv_spec(vc, 2),
                  meta_spec(kc), meta_spec(vc),
                  pl.BlockSpec((1, 2, Q, 3 * Q), lambda hp, b, i: (variant(i), hp, 0, 0))],
        out_specs=pl.BlockSpec((Q, w2), lambda hp, b, i: (b * nqb + i, hp)),
        out_shape=jax.ShapeDtypeStruct((B * Lp, H * hd), BF16),
        compiler_params=_cparams(("parallel", "parallel", "arbitrary")),
        name="nbhd_attention",
    )(proj, proj, proj, proj, proj, proj, proj, proj, proj, bias_tab)


def _mix_body(oda_ref, ona_ref, gda_ref, gna_ref, h_ref, wbd_ref, wbn_ref, wo_ref, g2_ref,
              h1_ref, xn2_ref):
    yda = jnp.dot(oda_ref[...], wbd_ref[...], preferred_element_type=F32)
    yna = jnp.dot(ona_ref[...], wbn_ref[...], preferred_element_type=F32)
    mixed = gda_ref[...].astype(F32) * yda + gna_ref[...].astype(F32) * yna
    h1 = h_ref[...] + jnp.dot(mixed.astype(BF16), wo_ref[...], preferred_element_type=F32)
    h1_ref[...] = h1
    ms = jnp.mean(h1 * h1, axis=-1, keepdims=True)
    xn2_ref[...] = (h1 * lax.rsqrt(ms + EPS) * g2_ref[...]).astype(xn2_ref.dtype)


def _mix(o_da, o_na, proj, h0, wbd, wbn, wo, g2, *, gda_col, gna_col, tm):
    T, D = h0.shape
    const = lambda shape: pl.BlockSpec(shape, lambda i: (0, 0), pipeline_mode=pl.Buffered(1))
    return pl.pallas_call(
        _mix_body,
        grid=(T // tm,),
        in_specs=[
            pl.BlockSpec((tm, o_da.shape[1]), lambda i: (i, 0)),
            pl.BlockSpec((tm, o_na.shape[1]), lambda i: (i, 0)),
            pl.BlockSpec((tm, D), lambda i: (i, gda_col // D)),
            pl.BlockSpec((tm, D), lambda i: (i, gna_col // D)),
            pl.BlockSpec((tm, D), lambda i: (i, 0)),
            const(wbd.shape), const(wbn.shape), const(wo.shape), const(g2.shape),
        ],
        out_specs=[pl.BlockSpec((tm, D), lambda i: (i, 0)),
                   pl.BlockSpec((tm, D), lambda i: (i, 0))],
        out_shape=[jax.ShapeDtypeStruct((T, D), F32), jax.ShapeDtypeStruct((T, D), BF16)],
        compiler_params=_cparams(("parallel",)),
        name="mix_out",
    )(o_da, o_na, proj, proj, h0, wbd, wbn, wo, g2)


def _cand_pairs(k):
    return [(i, j) for i in range(k) for j in range(k) if (i + 1) * (j + 1) <= k]


def _topk_cols(s, k):
    n = s.shape[0]
    rows = lax.broadcasted_iota(jnp.int32, s.shape, 0).astype(F32)
    rank = jnp.full(s.shape, float(k), F32)
    vals = []
    for r in range(k):
        m = jnp.max(s, axis=0, keepdims=True)
        first = jnp.min(jnp.where(s == m, rows, float(n)), axis=0, keepdims=True)
        hit = rows == first
        rank = jnp.where(hit, float(r), rank)
        s = jnp.where(hit, -jnp.inf, s)
        vals.append(m)
    return vals, rank


def _route_body(xn_ref, wq_ref, sk_ref, r2_ref, e2_ref, jn_ref, e1_ref, q_sc, *, heads, nk, topk):
    q = jnp.dot(xn_ref[...], wq_ref[...], preferred_element_type=F32).astype(BF16)
    for h in range(heads):
        q_sc[h] = q[:, h * 2 * nk:(h + 1) * 2 * nk]
    pairs = _cand_pairs(topk)
    npad = -(-len(pairs) // 8) * 8

    def per_head(h, _):
        qh = q_sc[h]
        tm = qh.shape[0]
        s1 = lax.dot_general(sk_ref[0], qh[:, :nk], _NT, preferred_element_type=F32)
        s2 = lax.dot_general(sk_ref[1], qh[:, nk:], _NT, preferred_element_type=F32)
        v1, rank1 = _topk_cols(s1, topk)
        v2, rank2 = _topk_cols(s2, topk)
        cand = jnp.concatenate([v1[i] + v2[j] for i, j in pairs]
                               + [jnp.full((npad - len(pairs), tm), -jnp.inf, F32)], axis=0)
        ex = jnp.exp(cand - (v1[0] + v2[0]))
        _, crank = _topk_cols(cand, topk)
        chosen = (crank < float(topk)).astype(F32)
        z = jnp.sum(chosen * ex, axis=0, keepdims=True)
        jn = jnp.zeros(s1.shape, F32)
        for i in range(topk):
            rows_i = [r for r, (pi, _) in enumerate(pairs) if pi == i]
            cnt = functools.reduce(lambda a, b: a + b, [chosen[r:r + 1] for r in rows_i])
            jn = jnp.where(rank1 == float(i), cnt, jn)
        r2_ref[h] = rank2
        e2_ref[h] = jnp.exp(s2 - v2[0])
        jn_ref[h] = jn
        e1_ref[h] = jnp.exp(s1 - v1[0]) / z
        return 0

    lax.fori_loop(0, heads, per_head, 0)


def _peer_route(xn2, wq, sk, *, heads, nk, topk, tm):
    T, D = xn2.shape
    body = functools.partial(_route_body, heads=heads, nk=nk, topk=topk)
    out = jax.ShapeDtypeStruct((heads, nk, T), F32)
    ospec = pl.BlockSpec((heads, nk, tm), lambda i: (0, 0, i))
    return pl.pallas_call(
        body,
        grid=(T // tm,),
        in_specs=[pl.BlockSpec((tm, D), lambda i: (i, 0)),
                  pl.BlockSpec(wq.shape, lambda i: (0, 0), pipeline_mode=pl.Buffered(1)),
                  pl.BlockSpec(sk.shape, lambda i: (0, 0, 0))],
        out_specs=[ospec] * 4,
        out_shape=[out] * 4,
        scratch_shapes=[pltpu.VMEM((heads, tm, 2 * nk), BF16)],
        compiler_params=_cparams(("parallel",)),
        name="peer_route",
    )(xn2, wq, sk)


def _gelu_tanh(x):
    return 0.5 * x * (1.0 + jnp.tanh(math.sqrt(2.0 / math.pi) * (x + 0.044715 * (x * x * x))))


def _peer_body(x_ref, u_ref, vt_ref, r2_ref, e2_ref, jn_ref, e1_ref, h1_ref, o_ref, acc_ref, *,
               heads, nk, nb):
    j = pl.program_id(1)

    @pl.when(j == 0)
    def _():
        acc_ref[...] = jnp.zeros_like(acc_ref)

    a = lax.dot_general(u_ref[...], x_ref[...], _NT, preferred_element_type=F32)
    act = _gelu_tanh(a)
    parts = []
    for nl in range(nb):
        n1 = j * nb + nl
        w = None
        for h in range(heads):
            jn = jn_ref[h, pl.ds(n1, 1), :]
            e1 = e1_ref[h, pl.ds(n1, 1), :]
            t = jnp.where(r2_ref[h] < jn, e2_ref[h] * e1, 0.0)
            w = t if w is None else w + t
        parts.append((w * act[nl * nk:(nl + 1) * nk]).astype(BF16))
    ht = jnp.concatenate(parts, axis=0)
    acc_ref[...] += jnp.dot(vt_ref[...], ht, preferred_element_type=F32)

    @pl.when(j == pl.num_programs(1) - 1)
    def _():
        o_ref[...] = h1_ref[...] + acc_ref[...].T


def _peer_experts(xn2, u_bf, vt_bf, r2, e2, jn, e1, h1, *, heads, nk, nb, tm):
    T, D = xn2.shape
    E = u_bf.shape[0]
    eb = nb * nk
    body = functools.partial(_peer_body, heads=heads, nk=nk, nb=nb)
    rspec = pl.BlockSpec((heads, nk, tm), lambda i, j: (0, 0, i))
    return pl.pallas_call(
        body,
        grid=(T // tm, E // eb),
        in_specs=[pl.BlockSpec((tm, D), lambda i, j: (i, 0)),
                  pl.BlockSpec((eb, D), lambda i, j: (j, 0)),
                  pl.BlockSpec((D, eb), lambda i, j: (0, j)),
                  rspec, rspec, rspec, rspec,
                  pl.BlockSpec((tm, D), lambda i, j: (i, 0))],
        out_specs=pl.BlockSpec((tm, D), lambda i, j: (i, 0)),
        out_shape=jax.ShapeDtypeStruct((T, D), F32),
        scratch_shapes=[pltpu.VMEM((D, tm), F32)],
        compiler_params=_cparams(("parallel", "arbitrary")),
        name="peer_experts",
    )(xn2, u_bf, vt_bf, r2, e2, jn, e1, h1)


def kernel(x, meta_tokens, norm1_g, w_in, da_q_norm_g, da_k_norm_g, lambda_q1, lambda_k1,
           lambda_q2, lambda_k2, da_out_norm_g, na_q_norm_g, na_k_norm_g, na_rpb,
           w_branch_da, w_branch_na, w_out, norm2_g, peer_w_query, peer_sub_keys, peer_u, peer_v):
    B, S, D = x.shape
    n_meta = meta_tokens.shape[0]
    depth = w_in.shape[0]
    da_hd = da_q_norm_g.shape[1]
    da_vw = w_branch_da.shape[1]
    da_heads = da_vw // (2 * da_hd)
    da_qkw = da_heads * 2 * da_hd
    na_hd = na_q_norm_g.shape[1]
    na_w = w_branch_na.shape[1]
    na_heads = na_rpb.shape[1]
    nk = peer_sub_keys.shape[2]
    peer_heads = peer_w_query.shape[2] // (2 * nk)
    splits = (da_qkw, da_qkw, da_vw, na_w, na_w, na_w, D, D)
    col = np.concatenate([[0], np.cumsum(splits)]).tolist()
    assert w_in.shape[2] == col[-1] and S % ROW_BLOCK == 0 and S % GRID_W == 0
    assert da_hd == na_hd and 2 * da_hd == LANES and nk == LANES

    Lp = -(-(S + n_meta) // ROW_BLOCK) * ROW_BLOCK
    T = B * Lp
    tn = 512
    tm_proj = max(t for t in (768, 512, 256) if T % t == 0)
    tm_peer = 512 if T % 512 == 0 else 256

    h = jnp.concatenate([x, jnp.broadcast_to(meta_tokens[None].astype(x.dtype), (B, n_meta, D)),
                         jnp.zeros((B, Lp - S - n_meta, D), x.dtype)], axis=1).reshape(T, D)
    slopes = jnp.asarray(2.0 ** (-8.0 * np.arange(1, da_heads + 1) / da_heads), dtype=F32)
    ones = lambda n: jnp.ones((n,), F32)

    for l in range(depth):
        gcols = jnp.concatenate([
            jnp.tile(da_q_norm_g[l].astype(F32), da_qkw // da_hd) * da_hd ** -0.5,
            jnp.tile(da_k_norm_g[l].astype(F32), da_qkw // da_hd),
            ones(da_vw),
            jnp.tile(na_q_norm_g[l].astype(F32), na_heads) * na_hd ** -0.5,
            jnp.tile(na_k_norm_g[l].astype(F32), na_heads),
            ones(na_w + 2 * D)])[None]
        norm_ranges = ((col[0] // tn, col[2] // tn), (col[3] // tn, col[5] // tn))
        proj = _inproj(h, norm1_g[l][None].astype(F32), w_in[l].astype(BF16), gcols,
                       norm_ranges=norm_ranges, sig_start=col[6] // tn, head_dim=da_hd,
                       tm=tm_proj, tn=tn)

        lam_init = 0.8 - 0.6 * math.exp(-0.3 * l)
        lamv = jnp.stack([lambda_q1[l], lambda_k1[l], lambda_q2[l], lambda_k2[l]]).astype(F32)
        o_da = _diff_attention(proj, slopes, lamv, da_out_norm_g[l][None].astype(F32),
                               B=B, Lp=Lp, S=S, n_meta=n_meta, H=da_heads, hd=da_hd,
                               q_col=col[0], k_col=col[1], v_col=col[2], lam_init=lam_init,
                               tq=ROW_BLOCK, tk=512 if S % 512 == 0 else ROW_BLOCK, tkm=LANES)

        qrows = ROW_BLOCK // GRID_W
        bias_tab = _na_bias_table(na_rpb[l], S // GRID_W, qrows)
        o_na = _nbhd_attention(proj, bias_tab, B=B, Lp=Lp, S=S, n_meta=n_meta, H=na_heads,
                               hd=na_hd, q_col=col[3], k_col=col[4], v_col=col[5], qrows=qrows)

        h1, xn2 = _mix(o_da, o_na, proj, h, w_branch_da[l].astype(BF16),
                       w_branch_na[l].astype(BF16), w_out[l].astype(BF16),
                       norm2_g[l][None].astype(F32), gda_col=col[6], gna_col=col[7], tm=ROW_BLOCK)

        r2, e2, jn, e1 = _peer_route(xn2, peer_w_query[l].astype(BF16),
                                     peer_sub_keys[l].astype(BF16), heads=peer_heads, nk=nk,
                                     topk=PEER_TOPK, tm=ROW_BLOCK)
        h = _peer_experts(xn2, peer_u[l].astype(BF16), peer_v[l].T.astype(BF16), r2, e2, jn, e1,
                          h1, heads=peer_heads, nk=nk, nb=4, tm=tm_peer)

    return h.reshape(B, Lp, D)[:, :S]
```

```python
import functools
import math

import numpy as np
import jax
import jax.numpy as jnp
from jax import lax
from jax.experimental import pallas as pl
from jax.experimental.pallas import tpu as pltpu

F32 = jnp.float32
BF16 = jnp.bfloat16

GRID_W = 64
NA_WIN_H = 8
NA_WIN_W = 16
PEER_TOPK = 16
EPS = 1e-6
NEG = -1e30

LANES = 128
MXU_DIM = 256
VMEM_LIMIT = 56 * 1024 * 1024

ROW_BLOCK = 256

_NT = (((1,), (1,)), ((), ()))


def _cparams(sem):
    return pltpu.CompilerParams(dimension_semantics=sem, vmem_limit_bytes=VMEM_LIMIT)


def _inproj_body(x_ref, g1_ref, w_ref, gc_ref, o_ref, xn_ref, *, norm_ranges, sig_start, head_dim):
    j = pl.program_id(1)

    @pl.when(j == 0)
    def _():
        x = x_ref[...]
        ms = jnp.mean(x * x, axis=-1, keepdims=True)
        xn_ref[...] = (x * lax.rsqrt(ms + EPS) * g1_ref[...]).astype(BF16)

    y = jnp.dot(xn_ref[...], w_ref[...], preferred_element_type=F32)
    tn = y.shape[1]

    is_norm = functools.reduce(jnp.logical_or, [(j >= a) & (j < b) for a, b in norm_ranges])
    is_sig = j >= sig_start

    @pl.when(is_norm)
    def _():
        r = lax.broadcasted_iota(jnp.int32, (MXU_DIM, MXU_DIM), 0) // head_dim
        c = lax.broadcasted_iota(jnp.int32, (MXU_DIM, MXU_DIM), 1) // head_dim
        ones_bd = (r == c).astype(BF16)
        yy = (y * y).astype(BF16)
        ss = jnp.concatenate(
            [jnp.dot(yy[:, t * MXU_DIM:(t + 1) * MXU_DIM], ones_bd, preferred_element_type=F32)
             for t in range(tn // MXU_DIM)], axis=1)
        o_ref[...] = (y * lax.rsqrt(ss * (1.0 / head_dim) + EPS) * gc_ref[...]).astype(o_ref.dtype)

    @pl.when(is_sig)
    def _():
        o_ref[...] = jax.nn.sigmoid(y).astype(o_ref.dtype)

    @pl.when(jnp.logical_not(is_norm | is_sig))
    def _():
        o_ref[...] = y.astype(o_ref.dtype)


def _inproj(h0, g1, w_bf, gcols, *, norm_ranges, sig_start, head_dim, tm, tn):
    T, D = h0.shape
    C = w_bf.shape[1]
    body = functools.partial(_inproj_body, norm_ranges=norm_ranges, sig_start=sig_start,
                             head_dim=head_dim)
    return pl.pallas_call(
        body,
        grid=(T // tm, C // tn),
        in_specs=[
            pl.BlockSpec((tm, D), lambda i, j: (i, 0)),
            pl.BlockSpec((1, D), lambda i, j: (0, 0)),
            pl.BlockSpec((D, tn), lambda i, j: (0, j)),
            pl.BlockSpec((1, tn), lambda i, j: (0, j)),
        ],
        out_specs=pl.BlockSpec((tm, tn), lambda i, j: (i, j)),
        out_shape=jax.ShapeDtypeStruct((T, C), BF16),
        scratch_shapes=[pltpu.VMEM((tm, D), BF16)],
        compiler_params=_cparams(("parallel", "arbitrary")),
        name="inproj",
    )(h0, g1, w_bf, gcols)


_VAR_BEFORE, _VAR_AFTER, _VAR_PLAIN, _VAR_MASK = 0, 1, 2, 3
_MASK_SCORE = 32768.0


def _split3(x):
    hi = x.astype(BF16).astype(F32)
    mid = (x - hi).astype(BF16).astype(F32)
    lo = (x - hi - mid).astype(BF16).astype(F32)
    return hi, mid, lo


def _extras(lane, base, first3, last3):
    out = jnp.zeros(lane.shape, F32)
    for t, val in enumerate(tuple(first3) + tuple(last3)):
        out = jnp.where(lane == base + t, val, out)
    return out


def _da_body(slopes_ref, q_ref, k_ref, v_ref, lamv_ref, og_ref, o_ref, kp_ref, qv_ref, sa_ref, sb_ref,
             *, S, n_meta, tq, nch, tkm, hd, lam_init):
    h = pl.program_id(1)
    i = pl.program_id(2)
    slope = slopes_ref[h]
    Lp = k_ref.shape[0]
    is_meta_q = i * tq >= S
    one = jnp.ones((), F32)

    @pl.when(i == 0)
    def _():
        k = k_ref[...]
        lane = lax.broadcasted_iota(jnp.int32, (Lp, LANES), 1)
        kidx = lax.broadcasted_iota(jnp.int32, (Lp, LANES), 0)
        kpos = jnp.where(kidx < S, kidx + n_meta, kidx - S).astype(F32)
        b3 = _split3(slope * kpos)
        kp_ref[0] = jnp.where(lane < hd, k, _extras(lane, hd, (one,) * 3, b3).astype(BF16))
        kp_ref[1] = jnp.where(lane >= hd, k, _extras(lane, 0, (one,) * 3, b3).astype(BF16))

    q = q_ref[...]
    lane = lax.broadcasted_iota(jnp.int32, (tq, LANES), 1)
    ridx = lax.broadcasted_iota(jnp.int32, (tq, LANES), 0)
    qidx = i * tq + ridx
    qpos = jnp.where(qidx < S, qidx + n_meta, qidx - S).astype(F32)
    sign = jnp.where(is_meta_q, -1.0, 1.0)
    a3 = _split3(-slope * qpos)
    zero = jnp.zeros((), F32)
    for c, base in ((0, hd), (1, 0)):
        keep = (lane < hd) if c == 0 else (lane >= hd)
        ext = sign * _extras(lane, base, a3, (one,) * 3)
        qv_ref[c, _VAR_BEFORE] = jnp.where(keep, q, ext.astype(BF16))
        qv_ref[c, _VAR_AFTER] = jnp.where(keep, q, (-ext).astype(BF16))
        qv_ref[c, _VAR_PLAIN] = jnp.where(keep, q, jnp.zeros_like(q))
        qv_ref[c, _VAR_MASK] = _extras(lane, base, (-_MASK_SCORE * one, zero, zero),
                                       (zero,) * 3).astype(BF16)
    ccol = lax.broadcasted_iota(jnp.int32, (tq, tq), 1)
    crow = lax.broadcasted_iota(jnp.int32, (tq, tq), 0)
    own_bias = jnp.where(i * tq + ccol < S + n_meta,
                         -slope * jnp.abs(crow - ccol).astype(F32), NEG)

    def update(carry_c, s_list, v):
        m, l, acc = carry_c
        tiles = lambda xs: [x[:, t * LANES:(t + 1) * LANES] for x in xs
                            for t in range(x.shape[1] // LANES)]
        m_new = jnp.maximum(m, jnp.max(functools.reduce(jnp.maximum, tiles(s_list)),
                                       axis=-1, keepdims=True))
        alpha = jnp.exp2(m - m_new)
        ps = [jnp.exp2(s - m_new) for s in s_list]
        l = alpha * l + jnp.sum(functools.reduce(lambda x, y: x + y, tiles(ps)),
                                axis=-1, keepdims=True)
        pcat = jnp.concatenate([p.astype(BF16) for p in ps], axis=1)
        acc = alpha * acc + jnp.dot(pcat, v, preferred_element_type=F32)
        return m_new, l, acc

    def scores(jb, dst_ref):
        for c in range(2):
            for kk in range(nch):
                g = jb * nch + kk
                var = jnp.where(g < i, _VAR_BEFORE, jnp.where(g == i, _VAR_MASK, _VAR_AFTER))
                kc = kp_ref[c, pl.ds(pl.multiple_of(g * tq, tq), tq), :]
                dst_ref[c, kk] = lax.dot_general(qv_ref[c, var], kc, _NT,
                                                 preferred_element_type=F32)

    def absorb(jb, src_ref, carry):
        v = v_ref[pl.ds(pl.multiple_of(jb * (nch * tq), nch * tq), nch * tq), :]
        return tuple(update(carry[c], [src_ref[c, kk] for kk in range(nch)], v) for c in range(2))

    nsb = S // (nch * tq)
    dv = v_ref.shape[1]
    carry = tuple((jnp.full((tq, 1), NEG, F32), jnp.zeros((tq, 1), F32), jnp.zeros((tq, dv), F32))
                  for _ in range(2))
    scores(0, sa_ref)

    def pair(t, carry):
        scores(2 * t + 1, sb_ref)
        carry = absorb(2 * t, sa_ref, carry)
        scores(2 * t + 2, sa_ref)
        return absorb(2 * t + 1, sb_ref, carry)

    carry = lax.fori_loop(0, nsb // 2 - 1, pair, carry)
    scores(nsb - 1, sb_ref)
    carry = absorb(nsb - 2, sa_ref, carry)

    mcol = lax.broadcasted_iota(jnp.int32, (tq, tkm), 1)
    meta_mask = jnp.where(mcol < n_meta, 0.0, NEG)
    var_m = jnp.where(is_meta_q, _VAR_MASK, _VAR_BEFORE)
    own_rows = pl.ds(pl.multiple_of(i * tq, tq), tq)
    tail_s = []
    for c in range(2):
        s_own = lax.dot_general(qv_ref[c, _VAR_PLAIN], kp_ref[c, own_rows, :], _NT,
                                preferred_element_type=F32) + own_bias
        s_meta = lax.dot_general(qv_ref[c, var_m], kp_ref[c, pl.ds(S, tkm), :], _NT,
                                 preferred_element_type=F32) + meta_mask
        tail_s.append([s_own, s_meta])
    carry = absorb(nsb - 1, sb_ref, carry)
    v_tail = jnp.concatenate([v_ref[own_rows, :], v_ref[pl.ds(S, tkm), :]], axis=0)
    carry = tuple(update(carry[c], tail_s[c], v_tail) for c in range(2))

    lv = lamv_ref[...]
    lam = (jnp.exp(jnp.sum(lv[0:1] * lv[1:2], axis=-1, keepdims=True))
           - jnp.exp(jnp.sum(lv[2:3] * lv[3:4], axis=-1, keepdims=True)) + lam_init)
    (_, l0, a0), (_, l1, a1) = carry
    o = a0 / l0 - lam * (a1 / l1)
    o = o * lax.rsqrt(jnp.mean(o * o, axis=-1, keepdims=True) + EPS) * og_ref[...] * (1.0 - lam_init)
    o_ref[...] = o.astype(o_ref.dtype)


def _diff_attention(proj, slopes, lamv, og, *, B, Lp, S, n_meta, H, hd, q_col, k_col, v_col,
                    lam_init, tq, nch, tkm):
    dv = 2 * hd
    nq = Lp // tq
    assert S % (2 * nch * tq) == 0 and dv == LANES
    body = functools.partial(_da_body, S=S, n_meta=n_meta, tq=tq, nch=nch, tkm=tkm, hd=hd,
                             lam_init=lam_init)
    qb, kb, vb = q_col // dv, k_col // dv, v_col // dv
    return pl.pallas_call(
        body,
        grid=(B, H, nq),
        in_specs=[
            pl.BlockSpec(memory_space=pltpu.SMEM),
            pl.BlockSpec((tq, dv), lambda b, h, i: (b * nq + i, qb + h)),
            pl.BlockSpec((Lp, dv), lambda b, h, i: (b, kb + h)),
            pl.BlockSpec((Lp, dv), lambda b, h, i: (b, vb + h)),
            pl.BlockSpec(lamv.shape, lambda b, h, i: (0, 0)),
            pl.BlockSpec((1, dv), lambda b, h, i: (0, 0)),
        ],
        out_specs=pl.BlockSpec((tq, dv), lambda b, h, i: (b * nq + i, h)),
        out_shape=jax.ShapeDtypeStruct((B * Lp, H * dv), BF16),
        scratch_shapes=[pltpu.VMEM((2, Lp, dv), BF16),
                        pltpu.VMEM((2, 4, tq, dv), BF16),
                        pltpu.VMEM((2, nch, tq, tq), F32),
                        pltpu.VMEM((2, nch, tq, tq), F32)],
        compiler_params=_cparams(("parallel", "parallel", "arbitrary")),
        name="diff_attention",
    )(slopes, proj, proj, proj, lamv, og)


def _na_bias_table(rpb, rows, qrows):
    W = GRID_W
    H, n_dr, n_dc = rpb.shape
    n_real = rows // qrows
    kh = min(NA_WIN_H, rows)
    qc = np.arange(W)[:, None]
    kc = np.arange(W)[None, :]
    cs = np.clip(qc - NA_WIN_W // 2, 0, W - NA_WIN_W)
    col_ok = (kc >= cs) & (kc < cs + NA_WIN_W)
    onehot = ((kc - qc + (NA_WIN_W - 1))[None] == np.arange(n_dc)[:, None, None]) & col_ok[None]
    tiles = jnp.einsum('hrd,dqk->hrqk', rpb.astype(F32), jnp.asarray(onehot, F32),
                       precision=lax.Precision.HIGHEST)
    tiles = jnp.where(jnp.asarray(col_ok), tiles, NEG)
    tiles = jnp.concatenate([tiles, jnp.full((H, 1, W, W), NEG, F32)], axis=1)
    sel = np.full((4, qrows, 3 * qrows), n_dr, np.int32)
    for v, (qb, kb0) in enumerate(((0, 0), (1, 0), (n_real - 1, n_real - 3))):
        r = (qb * qrows + np.arange(qrows))[:, None]
        kr = (kb0 * qrows + np.arange(3 * qrows))[None, :]
        r0 = np.clip(r - kh // 2, 0, rows - kh)
        row_ok = (kr >= r0) & (kr < r0 + kh)
        sel[v] = np.where(row_ok, kr - r + (NA_WIN_H - 1), n_dr)
    g = jnp.take(tiles, jnp.asarray(sel.reshape(-1)), axis=1)
    g = g.reshape(H, 4, qrows, 3 * qrows, W, W).transpose(1, 0, 2, 4, 3, 5)
    return g.reshape(4, H, qrows * W, 3 * qrows * W)


def _na_body(q_ref, k0_ref, k1_ref, k2_ref, v0_ref, v1_ref, v2_ref, km_ref, vm_ref, bias_ref,
             o_ref, *, hd, n_meta):
    q = q_ref[...]
    k = jnp.concatenate([k0_ref[...], k1_ref[...], k2_ref[...]], axis=0)
    v = jnp.concatenate([v0_ref[...], v1_ref[...], v2_ref[...]], axis=0)
    km = km_ref[...]
    vm = vm_ref[...]
    lane = lax.broadcasted_iota(jnp.int32, q.shape, 1)
    zero = jnp.zeros_like(q)
    midx = lax.broadcasted_iota(jnp.int32, (1, km.shape[0]), 1)
    mmask = jnp.where(midx < n_meta, 0.0, NEG)
    outs = []
    for hh in range(2):
        qh = jnp.where((lane >= hh * hd) & (lane < (hh + 1) * hd), q, zero)
        s = lax.dot_general(qh, k, _NT, preferred_element_type=F32) + bias_ref[0, hh]
        sm = lax.dot_general(qh, km, _NT, preferred_element_type=F32) + mmask
        m = jnp.maximum(jnp.max(s, axis=-1, keepdims=True), jnp.max(sm, axis=-1, keepdims=True))
        p = jnp.exp(s - m)
        pm = jnp.exp(sm - m)
        l = jnp.sum(p, axis=-1, keepdims=True) + jnp.sum(pm, axis=-1, keepdims=True)
        o = (jnp.dot(p.astype(BF16), v, preferred_element_type=F32)
             + jnp.dot(pm.astype(BF16), vm, preferred_element_type=F32))
        outs.append(o / l)
    o_ref[...] = jnp.where(lane < hd, outs[0], outs[1]).astype(o_ref.dtype)


def _nbhd_attention(proj, bias_tab, *, B, Lp, S, n_meta, H, hd, q_col, k_col, v_col, qrows):
    Q = qrows * GRID_W
    nqb = Lp // Q
    n_real = S // Q
    w2 = 2 * hd
    qc, kc, vc = q_col // w2, k_col // w2, v_col // w2
    mrows = LANES

    def kb0(i):
        return jnp.clip(i - 1, 0, n_real - 3)

    def variant(i):
        return jnp.where(i == 0, 0, jnp.where(i == n_real - 1, 2, jnp.where(i >= n_real, 3, 1)))

    def kv_spec(col, d):
        return pl.BlockSpec((Q, w2), lambda hp, b, i: (b * nqb + kb0(i) + d, col + hp))

    def meta_spec(col):
        return pl.BlockSpec((mrows, w2), lambda hp, b, i: ((b * Lp + S) // mrows, col + hp))

    body = functools.partial(_na_body, hd=hd, n_meta=n_meta)
    return pl.pallas_call(
        body,
        grid=(H // 2, B, nqb),
        in_specs=[pl.BlockSpec((Q, w2), lambda hp, b, i: (b * nqb + i, qc + hp)),
                  kv_spec(kc, 0), kv_spec(kc, 1), kv_spec(kc, 2),
                  kv_spec(vc, 0), kv_spec(vc, 1), kv_spec(vc, 2),
                  meta_spec(kc), meta_spec(vc),
                  pl.BlockSpec((1, 2, Q, 3 * Q), lambda hp, b, i: (variant(i), hp, 0, 0))],
        out_specs=pl.BlockSpec((Q, w2), lambda hp, b, i: (b * nqb + i, hp)),
        out_shape=jax.ShapeDtypeStruct((B * Lp, H * hd), BF16),
        compiler_params=_cparams(("parallel", "parallel", "arbitrary")),
        name="nbhd_attention",
    )(proj, proj, proj, proj, proj, proj, proj, proj, proj, bias_tab)


def _mix_body(oda_ref, ona_ref, gda_ref, gna_ref, h_ref, wbd_ref, wbn_ref, wo_ref, g2_ref,
              h1_ref, xn2_ref):
    yda = jnp.dot(oda_ref[...], wbd_ref[...], preferred_element_type=F32)
    yna = jnp.dot(ona_ref[...], wbn_ref[...], preferred_element_type=F32)
    mixed = gda_ref[...].astype(F32) * yda + gna_ref[...].astype(F32) * yna
    h1 = h_ref[...] + jnp.dot(mixed.astype(BF16), wo_ref[...], preferred_element_type=F32)
    h1_ref[...] = h1
    ms = jnp.mean(h1 * h1, axis=-1, keepdims=True)
    xn2_ref[...] = (h1 * lax.rsqrt(ms + EPS) * g2_ref[...]).astype(xn2_ref.dtype)


def _mix(o_da, o_na, proj, h0, wbd, wbn, wo, g2, *, gda_col, gna_col, tm):
    T, D = h0.shape
    const = lambda shape: pl.BlockSpec(shape, lambda i: (0, 0), pipeline_mode=pl.Buffered(1))
    return pl.pallas_call(
        _mix_body,
        grid=(T // tm,),
        in_specs=[
            pl.BlockSpec((tm, o_da.shape[1]), lambda i: (i, 0)),
            pl.BlockSpec((tm, o_na.shape[1]), lambda i: (i, 0)),
            pl.BlockSpec((tm, D), lambda i: (i, gda_col // D)),
            pl.BlockSpec((tm, D), lambda i: (i, gna_col // D)),
            pl.BlockSpec((tm, D), lambda i: (i, 0)),
            const(wbd.shape), const(wbn.shape), const(wo.shape), const(g2.shape),
        ],
        out_specs=[pl.BlockSpec((tm, D), lambda i: (i, 0)),
                   pl.BlockSpec((tm, D), lambda i: (i, 0))],
        out_shape=[jax.ShapeDtypeStruct((T, D), F32), jax.ShapeDtypeStruct((T, D), BF16)],
        compiler_params=_cparams(("parallel",)),
        name="mix_out",
    )(o_da, o_na, proj, proj, h0, wbd, wbn, wo, g2)


def _cand_pairs(k):
    return [(i, j) for i in range(k) for j in range(k) if (i + 1) * (j + 1) <= k]


def _topk_cols(s, k):
    n = s.shape[0]
    rows = lax.broadcasted_iota(jnp.int32, s.shape, 0).astype(F32)
    rank = jnp.full(s.shape, float(k), F32)
    vals = []
    for r in range(k):
        m = jnp.max(s, axis=0, keepdims=True)
        first = jnp.min(jnp.where(s == m, rows, float(n)), axis=0, keepdims=True)
        hit = rows == first
        rank = jnp.where(hit, float(r), rank)
        s = jnp.where(hit, -jnp.inf, s)
        vals.append(m)
    return vals, rank


def _route_body(xn_ref, wq_ref, sk_ref, r2_ref, e2_ref, jn_ref, e1_ref, q_sc, *, heads, nk, topk):
    q = jnp.dot(xn_ref[...], wq_ref[...], preferred_element_type=F32).astype(BF16)
    for h in range(heads):
        q_sc[h] = q[:, h * 2 * nk:(h + 1) * 2 * nk]
    pairs = _cand_pairs(topk)
    npad = -(-len(pairs) // 8) * 8

    def per_head(h, _):
        qh = q_sc[h]
        tm = qh.shape[0]
        s1 = lax.dot_general(sk_ref[0], qh[:, :nk], _NT, preferred_element_type=F32)
        s2 = lax.dot_general(sk_ref[1], qh[:, nk:], _NT, preferred_element_type=F32)
        v1, rank1 = _topk_cols(s1, topk)
        v2, rank2 = _topk_cols(s2, topk)
        cand = jnp.concatenate([v1[i] + v2[j] for i, j in pairs]
                               + [jnp.full((npad - len(pairs), tm), -jnp.inf, F32)], axis=0)
        ex = jnp.exp(cand - (v1[0] + v2[0]))
        _, crank = _topk_cols(cand, topk)
        chosen = (crank < float(topk)).astype(F32)
        z = jnp.sum(chosen * ex, axis=0, keepdims=True)
        jn = jnp.zeros(s1.shape, F32)
        for i in range(topk):
            rows_i = [r for r, (pi, _) in enumerate(pairs) if pi == i]
            cnt = functools.reduce(lambda a, b: a + b, [chosen[r:r + 1] for r in rows_i])
            jn = jnp.where(rank1 == float(i), cnt, jn)
        r2_ref[h] = rank2.astype(r2_ref.dtype)
        e2_ref[h] = jnp.exp(s2 - v2[0]).astype(e2_ref.dtype)
        jn_ref[h] = jn
        e1_ref[h] = jnp.exp(s1 - v1[0]) / z
        return 0

    lax.fori_loop(0, heads, per_head, 0)


def _peer_route(xn2, wq, sk, *, heads, nk, topk, tm):
    T, D = xn2.shape
    body = functools.partial(_route_body, heads=heads, nk=nk, topk=topk)
    out_f32 = jax.ShapeDtypeStruct((heads, nk, T), F32)
    out_bf16 = jax.ShapeDtypeStruct((heads, nk, T), BF16)
    ospec = pl.BlockSpec((heads, nk, tm), lambda i: (0, 0, i))
    return pl.pallas_call(
        body,
        grid=(T // tm,),
        in_specs=[pl.BlockSpec((tm, D), lambda i: (i, 0)),
                  pl.BlockSpec(wq.shape, lambda i: (0, 0), pipeline_mode=pl.Buffered(1)),
                  pl.BlockSpec(sk.shape, lambda i: (0, 0, 0))],
        out_specs=[ospec] * 4,
        out_shape=[out_bf16, out_bf16, out_f32, out_f32],
        scratch_shapes=[pltpu.VMEM((heads, tm, 2 * nk), BF16)],
        compiler_params=_cparams(("parallel",)),
        name="peer_route",
    )(xn2, wq, sk)


def _gelu_tanh(x):
    return 0.5 * x * (1.0 + jnp.tanh(math.sqrt(2.0 / math.pi) * (x + 0.044715 * (x * x * x))))


def _peer_body(x_ref, u_ref, vt_ref, r2_ref, e2_ref, jn_ref, e1_ref, o_ref, acc_ref, *,
               heads, nk, nb, cw):
    j = pl.program_id(1)
    x = x_ref[...]
    nchunk = nb * nk // cw

    def scores(c):
        return lax.dot_general(u_ref[c * cw:(c + 1) * cw, :], x, _NT, preferred_element_type=F32)

    a_next = scores(0)
    hts = []
    for c in range(nchunk):
        a = a_next
        if c + 1 < nchunk:
            a_next = scores(c + 1)
        act = _gelu_tanh(a).astype(BF16)
        ws = []
        for nl in range(cw // nk):
            n1 = j * nb + c * (cw // nk) + nl
            w = None
            for h in range(heads):
                jn = jn_ref[h, pl.ds(n1, 1), :].astype(BF16)
                e1 = e1_ref[h, pl.ds(n1, 1), :].astype(BF16)
                t = jnp.where(r2_ref[h] < jn, e2_ref[h] * e1, jnp.zeros((), BF16))
                w = t if w is None else w + t
            ws.append(w)
        hts.append(jnp.concatenate(ws, axis=0) * act)
    total = jnp.dot(vt_ref[...], jnp.concatenate(hts, axis=0), preferred_element_type=F32)

    @pl.when(j == 0)
    def _():
        acc_ref[...] = total

    @pl.when(j > 0)
    def _():
        acc_ref[...] += total

    @pl.when(j == pl.num_programs(1) - 1)
    def _():
        o_ref[...] = acc_ref[...].T


def _peer_experts(xn2, u_bf, vt_bf, r2, e2, jn, e1, *, heads, nk, nb, tm):
    T, D = xn2.shape
    E = u_bf.shape[0]
    eb = nb * nk
    body = functools.partial(_peer_body, heads=heads, nk=nk, nb=nb, cw=MXU_DIM)
    rspec = pl.BlockSpec((heads, nk, tm), lambda i, j: (0, 0, i))
    return pl.pallas_call(
        body,
        grid=(T // tm, E // eb),
        in_specs=[pl.BlockSpec((tm, D), lambda i, j: (i, 0)),
                  pl.BlockSpec((eb, D), lambda i, j: (j, 0)),
                  pl.BlockSpec((D, eb), lambda i, j: (0, j)),
                  rspec, rspec, rspec, rspec],
        out_specs=pl.BlockSpec((tm, D), lambda i, j: (i, 0)),
        out_shape=jax.ShapeDtypeStruct((T, D), F32),
        scratch_shapes=[pltpu.VMEM((D, tm), F32)],
        compiler_params=_cparams(("parallel", "arbitrary")),
        name="peer_experts",
    )(xn2, u_bf, vt_bf, r2, e2, jn, e1)


def kernel(x, meta_tokens, norm1_g, w_in, da_q_norm_g, da_k_norm_g, lambda_q1, lambda_k1,
           lambda_q2, lambda_k2, da_out_norm_g, na_q_norm_g, na_k_norm_g, na_rpb,
           w_branch_da, w_branch_na, w_out, norm2_g, peer_w_query, peer_sub_keys, peer_u, peer_v):
    B, S, D = x.shape
    n_meta = meta_tokens.shape[0]
    depth = w_in.shape[0]
    da_hd = da_q_norm_g.shape[1]
    da_vw = w_branch_da.shape[1]
    da_heads = da_vw // (2 * da_hd)
    da_qkw = da_heads * 2 * da_hd
    na_hd = na_q_norm_g.shape[1]
    na_w = w_branch_na.shape[1]
    na_heads = na_rpb.shape[1]
    nk = peer_sub_keys.shape[2]
    peer_heads = peer_w_query.shape[2] // (2 * nk)
    splits = (da_qkw, da_qkw, da_vw, na_w, na_w, na_w, D, D)
    col = np.concatenate([[0], np.cumsum(splits)]).tolist()
    assert w_in.shape[2] == col[-1] and S % ROW_BLOCK == 0 and S % GRID_W == 0
    assert da_hd == na_hd and 2 * da_hd == LANES and nk == LANES

    Lp = -(-(S + n_meta) // ROW_BLOCK) * ROW_BLOCK
    T = B * Lp
    tn = 512
    tm_proj = max(t for t in (768, 512, 256) if T % t == 0)
    tm_peer = 512 if T % 512 == 0 else 256

    h = jnp.concatenate([x, jnp.broadcast_to(meta_tokens[None].astype(x.dtype), (B, n_meta, D)),
                         jnp.zeros((B, Lp - S - n_meta, D), x.dtype)], axis=1).reshape(T, D)
    log2e = math.log2(math.e)
    slopes = jnp.asarray(2.0 ** (-8.0 * np.arange(1, da_heads + 1) / da_heads) * log2e, dtype=F32)
    ones = lambda n: jnp.ones((n,), F32)

    for l in range(depth):
        gcols = jnp.concatenate([
            jnp.tile(da_q_norm_g[l].astype(F32), da_qkw // da_hd) * (da_hd ** -0.5 * log2e),
            jnp.tile(da_k_norm_g[l].astype(F32), da_qkw // da_hd),
            ones(da_vw),
            jnp.tile(na_q_norm_g[l].astype(F32), na_heads) * na_hd ** -0.5,
            jnp.tile(na_k_norm_g[l].astype(F32), na_heads),
            ones(na_w + 2 * D)])[None]
        norm_ranges = ((col[0] // tn, col[2] // tn), (col[3] // tn, col[5] // tn))
        proj = _inproj(h, norm1_g[l][None].astype(F32), w_in[l].astype(BF16), gcols,
                       norm_ranges=norm_ranges, sig_start=col[6] // tn, head_dim=da_hd,
                       tm=tm_proj, tn=tn)

        lam_init = 0.8 - 0.6 * math.exp(-0.3 * l)
        lamv = jnp.stack([lambda_q1[l], lambda_k1[l], lambda_q2[l], lambda_k2[l]]).astype(F32)
        o_da = _diff_attention(proj, slopes, lamv, da_out_norm_g[l][None].astype(F32),
                               B=B, Lp=Lp, S=S, n_meta=n_meta, H=da_heads, hd=da_hd,
                               q_col=col[0], k_col=col[1], v_col=col[2], lam_init=lam_init,
                               tq=ROW_BLOCK, nch=4, tkm=LANES)

        qrows = ROW_BLOCK // GRID_W
        bias_tab = _na_bias_table(na_rpb[l], S // GRID_W, qrows)
        o_na = _nbhd_attention(proj, bias_tab, B=B, Lp=Lp, S=S, n_meta=n_meta, H=na_heads,
                               hd=na_hd, q_col=col[3], k_col=col[4], v_col=col[5], qrows=qrows)

        h1, xn2 = _mix(o_da, o_na, proj, h, w_branch_da[l].astype(BF16),
                       w_branch_na[l].astype(BF16), w_out[l].astype(BF16),
                       norm2_g[l][None].astype(F32), gda_col=col[6], gna_col=col[7], tm=ROW_BLOCK)

        r2, e2, jn, e1 = _peer_route(xn2, peer_w_query[l].astype(BF16),
                                     peer_sub_keys[l].astype(BF16), heads=peer_heads, nk=nk,
                                     topk=PEER_TOPK, tm=ROW_BLOCK)
        ffn = _peer_experts(xn2, peer_u[l].astype(BF16), peer_v[l].T.astype(BF16), r2, e2, jn, e1,
                            heads=peer_heads, nk=nk, nb=8, tm=tm_peer)
        h = h1 + ffn

    return h.reshape(B, Lp, D)[:, :S]
```

```python
import functools
import math

import numpy as np
import jax
import jax.numpy as jnp
from jax import lax
from jax.experimental import pallas as pl
from jax.experimental.pallas import tpu as pltpu

F32 = jnp.float32
BF16 = jnp.bfloat16

GRID_W = 64
NA_WIN_H = 8
NA_WIN_W = 16
PEER_TOPK = 16
EPS = 1e-6
NEG = -1e30

LANES = 128
MXU_DIM = 256
VMEM_LIMIT = 56 * 1024 * 1024

ROW_BLOCK = 256
META_ROWS = LANES

_NT = (((1,), (1,)), ((), ()))


def _cparams(sem):
    return pltpu.CompilerParams(dimension_semantics=sem, vmem_limit_bytes=VMEM_LIMIT)


def _inproj_body(x_ref, g1_ref, w_ref, gc_ref, o_ref, xn_ref, *, norm_ranges, sig_start, head_dim):
    j = pl.program_id(1)

    @pl.when(j == 0)
    def _():
        x = x_ref[...]
        ms = jnp.mean(x * x, axis=-1, keepdims=True)
        xn_ref[...] = (x * lax.rsqrt(ms + EPS) * g1_ref[...]).astype(BF16)

    tn = o_ref.shape[1]
    chunks = [slice(t * MXU_DIM, (t + 1) * MXU_DIM) for t in range(tn // MXU_DIM)]

    def cols(sl):
        return jnp.dot(xn_ref[...], w_ref[:, sl], preferred_element_type=F32)

    is_norm = functools.reduce(jnp.logical_or, [(j >= a) & (j < b) for a, b in norm_ranges])
    is_sig = j >= sig_start

    @pl.when(is_norm)
    def _():
        r = lax.broadcasted_iota(jnp.int32, (MXU_DIM, MXU_DIM), 0) // head_dim
        c = lax.broadcasted_iota(jnp.int32, (MXU_DIM, MXU_DIM), 1) // head_dim
        ones_bd = (r == c).astype(BF16)
        y = jnp.dot(xn_ref[...], w_ref[...], preferred_element_type=F32)
        yy = (y * y).astype(BF16)
        ss = jnp.concatenate([jnp.dot(yy[:, sl], ones_bd, preferred_element_type=F32)
                              for sl in chunks], axis=1)
        o_ref[...] = (y * lax.rsqrt(ss * (1.0 / head_dim) + EPS) * gc_ref[...]).astype(o_ref.dtype)

    @pl.when(is_sig)
    def _():
        for sl in chunks:
            o_ref[:, sl] = jax.nn.sigmoid(cols(sl)).astype(o_ref.dtype)

    @pl.when(jnp.logical_not(is_norm | is_sig))
    def _():
        for sl in chunks:
            o_ref[:, sl] = cols(sl).astype(o_ref.dtype)


def _inproj(x2d, g1, w_bf, gcols, *, norm_ranges, sig_start, head_dim, tm, tn):
    T, D = x2d.shape
    C = w_bf.shape[1]
    body = functools.partial(_inproj_body, norm_ranges=norm_ranges, sig_start=sig_start,
                             head_dim=head_dim)
    return pl.pallas_call(
        body,
        grid=(T // tm, C // tn),
        in_specs=[
            pl.BlockSpec((tm, D), lambda i, j: (i, 0)),
            pl.BlockSpec((1, D), lambda i, j: (0, 0)),
            pl.BlockSpec((D, tn), lambda i, j: (0, j)),
            pl.BlockSpec((1, tn), lambda i, j: (0, j)),
        ],
        out_specs=pl.BlockSpec((tm, tn), lambda i, j: (i, j)),
        out_shape=jax.ShapeDtypeStruct((T, C), BF16),
        scratch_shapes=[pltpu.VMEM((tm, D), BF16)],
        compiler_params=_cparams(("parallel", "arbitrary")),
        name="inproj",
    )(x2d, g1, w_bf, gcols)


_VAR_BEFORE, _VAR_AFTER, _VAR_PLAIN, _VAR_MASK = 0, 1, 2, 3
_MASK_SCORE = 32768.0


def _split3(x):
    hi = x.astype(BF16).astype(F32)
    mid = (x - hi).astype(BF16).astype(F32)
    lo = (x - hi - mid).astype(BF16).astype(F32)
    return hi, mid, lo


def _extras(lane, base, first3, last3):
    out = jnp.zeros(lane.shape, F32)
    for t, val in enumerate(tuple(first3) + tuple(last3)):
        out = jnp.where(lane == base + t, val, out)
    return out


def _da_body(slopes_ref, q_ref, k_ref, v_ref, km_ref, vm_ref, lamv_ref, og_ref, o_ref,
             kp_ref, qv_ref, sa_ref, sb_ref, *, n_meta, tq, nch, hd, lam_init):
    h = pl.program_id(1)
    i = pl.program_id(2)
    slope = slopes_ref[h]
    S = k_ref.shape[0]
    tkm = km_ref.shape[0]
    one = jnp.ones((), F32)
    zero = jnp.zeros((), F32)

    @pl.when(i == 0)
    def _():
        for rows, src, pos0 in ((pl.ds(0, S), k_ref, n_meta), (pl.ds(S, tkm), km_ref, 0)):
            k = src[...]
            lane = lax.broadcasted_iota(jnp.int32, k.shape, 1)
            kpos = (lax.broadcasted_iota(jnp.int32, k.shape, 0) + pos0).astype(F32)
            b3 = _split3(slope * kpos)
            kp_ref[0, rows, :] = jnp.where(lane < hd, k, _extras(lane, hd, (one,) * 3, b3).astype(BF16))
            kp_ref[1, rows, :] = jnp.where(lane >= hd, k, _extras(lane, 0, (one,) * 3, b3).astype(BF16))

    q = q_ref[...]
    lane = lax.broadcasted_iota(jnp.int32, (tq, LANES), 1)
    qpos = (i * tq + n_meta + lax.broadcasted_iota(jnp.int32, (tq, LANES), 0)).astype(F32)
    a3 = _split3(-slope * qpos)
    for c, base in ((0, hd), (1, 0)):
        keep = (lane < hd) if c == 0 else (lane >= hd)
        ext = _extras(lane, base, a3, (one,) * 3)
        qv_ref[c, _VAR_BEFORE] = jnp.where(keep, q, ext.astype(BF16))
        qv_ref[c, _VAR_AFTER] = jnp.where(keep, q, (-ext).astype(BF16))
        qv_ref[c, _VAR_PLAIN] = jnp.where(keep, q, jnp.zeros_like(q))
        qv_ref[c, _VAR_MASK] = _extras(lane, base, (-_MASK_SCORE * one, zero, zero),
                                       (zero,) * 3).astype(BF16)
    ccol = lax.broadcasted_iota(jnp.int32, (tq, tq), 1)
    crow = lax.broadcasted_iota(jnp.int32, (tq, tq), 0)
    own_bias = -slope * jnp.abs(crow - ccol).astype(F32)

    def update(carry_c, s_list, v):
        m, l, acc = carry_c
        tiles = lambda xs: [x[:, t * LANES:(t + 1) * LANES] for x in xs
                            for t in range(x.shape[1] // LANES)]
        m_new = jnp.maximum(m, jnp.max(functools.reduce(jnp.maximum, tiles(s_list)),
                                       axis=-1, keepdims=True))
        alpha = jnp.exp2(m - m_new)
        ps = [jnp.exp2(s - m_new) for s in s_list]
        l = alpha * l + jnp.sum(functools.reduce(lambda x, y: x + y, tiles(ps)),
                                axis=-1, keepdims=True)
        pcat = jnp.concatenate([p.astype(BF16) for p in ps], axis=1)
        acc = alpha * acc + jnp.dot(pcat, v, preferred_element_type=F32)
        return m_new, l, acc

    def scores(jb, dst_ref):
        for c in range(2):
            for kk in range(nch):
                g = jb * nch + kk
                var = jnp.where(g < i, _VAR_BEFORE, jnp.where(g == i, _VAR_MASK, _VAR_AFTER))
                kc = kp_ref[c, pl.ds(pl.multiple_of(g * tq, tq), tq), :]
                dst_ref[c, kk] = lax.dot_general(qv_ref[c, var], kc, _NT,
                                                 preferred_element_type=F32)

    def absorb(jb, src_ref, carry):
        v = v_ref[pl.ds(pl.multiple_of(jb * (nch * tq), nch * tq), nch * tq), :]
        return tuple(update(carry[c], [src_ref[c, kk] for kk in range(nch)], v) for c in range(2))

    nsb = S // (nch * tq)
    dv = v_ref.shape[1]
    carry = tuple((jnp.full((tq, 1), NEG, F32), jnp.zeros((tq, 1), F32), jnp.zeros((tq, dv), F32))
                  for _ in range(2))
    scores(0, sa_ref)

    def pair(t, carry):
        scores(2 * t + 1, sb_ref)
        carry = absorb(2 * t, sa_ref, carry)
        scores(2 * t + 2, sa_ref)
        return absorb(2 * t + 1, sb_ref, carry)

    carry = lax.fori_loop(0, nsb // 2 - 1, pair, carry)
    scores(nsb - 1, sb_ref)
    carry = absorb(nsb - 2, sa_ref, carry)

    meta_mask = jnp.where(lax.broadcasted_iota(jnp.int32, (tq, tkm), 1) < n_meta, 0.0, NEG)
    own_rows = pl.ds(pl.multiple_of(i * tq, tq), tq)
    tail_s = []
    for c in range(2):
        s_own = lax.dot_general(qv_ref[c, _VAR_PLAIN], kp_ref[c, own_rows, :], _NT,
                                preferred_element_type=F32) + own_bias
        s_meta = lax.dot_general(qv_ref[c, _VAR_BEFORE], kp_ref[c, pl.ds(S, tkm), :], _NT,
                                 preferred_element_type=F32) + meta_mask
        tail_s.append([s_own, s_meta])
    carry = absorb(nsb - 1, sb_ref, carry)
    v_tail = jnp.concatenate([v_ref[own_rows, :], vm_ref[...]], axis=0)
    carry = tuple(update(carry[c], tail_s[c], v_tail) for c in range(2))

    lv = lamv_ref[...]
    lam = (jnp.exp(jnp.sum(lv[0:1] * lv[1:2], axis=-1, keepdims=True))
           - jnp.exp(jnp.sum(lv[2:3] * lv[3:4], axis=-1, keepdims=True)) + lam_init)
    (_, l0, a0), (_, l1, a1) = carry
    o = a0 / l0 - lam * (a1 / l1)
    o = o * lax.rsqrt(jnp.mean(o * o, axis=-1, keepdims=True) + EPS) * og_ref[...] * (1.0 - lam_init)
    o_ref[...] = o.astype(o_ref.dtype)


def _diff_attention(proj, projm, slopes, lamv, og, *, B, S, n_meta, H, hd, q_col, k_col, v_col,
                    lam_init, tq, nch):
    dv = 2 * hd
    nq = S // tq
    assert S % (2 * nch * tq) == 0 and dv == LANES
    body = functools.partial(_da_body, n_meta=n_meta, tq=tq, nch=nch, hd=hd, lam_init=lam_init)
    qb, kb, vb = q_col // dv, k_col // dv, v_col // dv
    tkm = projm.shape[0]
    return pl.pallas_call(
        body,
        grid=(B, H, nq),
        in_specs=[
            pl.BlockSpec(memory_space=pltpu.SMEM),
            pl.BlockSpec((tq, dv), lambda b, h, i: (b * nq + i, qb + h)),
            pl.BlockSpec((S, dv), lambda b, h, i: (b, kb + h)),
            pl.BlockSpec((S, dv), lambda b, h, i: (b, vb + h)),
            pl.BlockSpec((tkm, dv), lambda b, h, i: (0, kb + h)),
            pl.BlockSpec((tkm, dv), lambda b, h, i: (0, vb + h)),
            pl.BlockSpec(lamv.shape, lambda b, h, i: (0, 0)),
            pl.BlockSpec((1, dv), lambda b, h, i: (0, 0)),
        ],
        out_specs=pl.BlockSpec((tq, dv), lambda b, h, i: (b * nq + i, h)),
        out_shape=jax.ShapeDtypeStruct((B * S, H * dv), BF16),
        scratch_shapes=[pltpu.VMEM((2, S + tkm, dv), BF16),
                        pltpu.VMEM((2, 4, tq, dv), BF16),
                        pltpu.VMEM((2, nch, tq, tq), F32),
                        pltpu.VMEM((2, nch, tq, tq), F32)],
        compiler_params=_cparams(("parallel", "parallel", "arbitrary")),
        name="diff_attention",
    )(slopes, proj, proj, proj, projm, projm, lamv, og)


def _na_bias_table(rpb, rows, qrows):
    W = GRID_W
    H, n_dr, n_dc = rpb.shape
    n_blk = rows // qrows
    kh = min(NA_WIN_H, rows)
    qc = np.arange(W)[:, None]
    kc = np.arange(W)[None, :]
    cs = np.clip(qc - NA_WIN_W // 2, 0, W - NA_WIN_W)
    col_ok = (kc >= cs) & (kc < cs + NA_WIN_W)
    onehot = ((kc - qc + (NA_WIN_W - 1))[None] == np.arange(n_dc)[:, None, None]) & col_ok[None]
    tiles = jnp.einsum('hrd,dqk->hrqk', rpb.astype(F32), jnp.asarray(onehot, F32),
                       precision=lax.Precision.HIGHEST)
    tiles = jnp.where(jnp.asarray(col_ok), tiles, NEG)
    tiles = jnp.concatenate([tiles, jnp.full((H, 1, W, W), NEG, F32)], axis=1)
    sel = np.full((3, qrows, 3 * qrows), n_dr, np.int32)
    for v, (qb, kb0) in enumerate(((0, 0), (1, 0), (n_blk - 1, n_blk - 3))):
        r = (qb * qrows + np.arange(qrows))[:, None]
        kr = (kb0 * qrows + np.arange(3 * qrows))[None, :]
        r0 = np.clip(r - kh // 2, 0, rows - kh)
        row_ok = (kr >= r0) & (kr < r0 + kh)
        sel[v] = np.where(row_ok, kr - r + (NA_WIN_H - 1), n_dr)
    g = jnp.take(tiles, jnp.asarray(sel.reshape(-1)), axis=1)
    g = g.reshape(H, 3, qrows, 3 * qrows, W, W).transpose(1, 0, 2, 4, 3, 5)
    return g.reshape(3, H, qrows * W, 3 * qrows * W)


def _na_body(q_ref, k0_ref, k1_ref, k2_ref, v0_ref, v1_ref, v2_ref, km_ref, vm_ref, bias_ref,
             o_ref, *, hd, n_meta, nh):
    q = q_ref[...]
    k = jnp.concatenate([k0_ref[...], k1_ref[...], k2_ref[...]], axis=0)
    v = jnp.concatenate([v0_ref[...], v1_ref[...], v2_ref[...]], axis=0)
    km = km_ref[...]
    vm = vm_ref[...]
    lane = lax.broadcasted_iota(jnp.int32, q.shape, 1)
    zero = jnp.zeros_like(q)
    midx = lax.broadcasted_iota(jnp.int32, (1, km.shape[0]), 1)
    mmask = jnp.where(midx < n_meta, 0.0, NEG)
    out = jnp.zeros(q.shape, F32)
    for hh in range(nh):
        mine = (lane >= hh * hd) & (lane < (hh + 1) * hd)
        qh = jnp.where(mine, q, zero)
        s = lax.dot_general(qh, k, _NT, preferred_element_type=F32) + bias_ref[0, hh]
        sm = lax.dot_general(qh, km, _NT, preferred_element_type=F32) + mmask
        m = jnp.maximum(jnp.max(s, axis=-1, keepdims=True), jnp.max(sm, axis=-1, keepdims=True))
        p = jnp.exp(s - m)
        pm = jnp.exp(sm - m)
        l = jnp.sum(p, axis=-1, keepdims=True) + jnp.sum(pm, axis=-1, keepdims=True)
        o = (jnp.dot(p.astype(BF16), v, preferred_element_type=F32)
             + jnp.dot(pm.astype(BF16), vm, preferred_element_type=F32))
        out = jnp.where(mine, o / l, out)
    o_ref[...] = out.astype(o_ref.dtype)


def _nbhd_attention(proj, projm, bias_tab, *, B, S, n_meta, H, hd, q_col, k_col, v_col, qrows, nh):
    Q = qrows * GRID_W
    nqb = S // Q
    wb = nh * hd
    qc, kc, vc = q_col // wb, k_col // wb, v_col // wb
    mrows = projm.shape[0]

    def kb0(i):
        return jnp.clip(i - 1, 0, nqb - 3)

    def variant(i):
        return jnp.where(i == 0, 0, jnp.where(i == nqb - 1, 2, 1))

    def kv_spec(col, d):
        return pl.BlockSpec((Q, wb), lambda hg, b, i: (b * nqb + kb0(i) + d, col + hg))

    def meta_spec(col):
        return pl.BlockSpec((mrows, wb), lambda hg, b, i: (0, col + hg))

    body = functools.partial(_na_body, hd=hd, n_meta=n_meta, nh=nh)
    return pl.pallas_call(
        body,
        grid=(H // nh, B, nqb),
        in_specs=[pl.BlockSpec((Q, wb), lambda hg, b, i: (b * nqb + i, qc + hg)),
                  kv_spec(kc, 0), kv_spec(kc, 1), kv_spec(kc, 2),
                  kv_spec(vc, 0), kv_spec(vc, 1), kv_spec(vc, 2),
                  meta_spec(kc), meta_spec(vc),
                  pl.BlockSpec((1, nh, Q, 3 * Q), lambda hg, b, i: (variant(i), hg, 0, 0))],
        out_specs=pl.BlockSpec((Q, wb), lambda hg, b, i: (b * nqb + i, hg)),
        out_shape=jax.ShapeDtypeStruct((B * S, H * hd), BF16),
        compiler_params=_cparams(("parallel", "parallel", "arbitrary")),
        name="nbhd_attention",
    )(proj, proj, proj, proj, proj, proj, proj, projm, projm, bias_tab)


def _mix_body(oda_ref, ona_ref, gda_ref, gna_ref, h_ref, wbd_ref, wbn_ref, wo_ref, g2_ref,
              h1_ref, xn2_ref):
    yda = jnp.dot(oda_ref[...], wbd_ref[...], preferred_element_type=F32)
    yna = jnp.dot(ona_ref[...], wbn_ref[...], preferred_element_type=F32)
    mixed = gda_ref[...].astype(F32) * yda + gna_ref[...].astype(F32) * yna
    h1 = h_ref[...] + jnp.dot(mixed.astype(BF16), wo_ref[...], preferred_element_type=F32)
    h1_ref[...] = h1
    ms = jnp.mean(h1 * h1, axis=-1, keepdims=True)
    xn2_ref[...] = (h1 * lax.rsqrt(ms + EPS) * g2_ref[...]).astype(xn2_ref.dtype)


def _mix(o_da, o_na, proj, h0, wbd, wbn, wo, g2, *, gda_col, gna_col, tm):
    T, D = h0.shape
    const = lambda shape: pl.BlockSpec(shape, lambda i: (0, 0), pipeline_mode=pl.Buffered(1))
    return pl.pallas_call(
        _mix_body,
        grid=(T // tm,),
        in_specs=[
            pl.BlockSpec((tm, o_da.shape[1]), lambda i: (i, 0)),
            pl.BlockSpec((tm, o_na.shape[1]), lambda i: (i, 0)),
            pl.BlockSpec((tm, D), lambda i: (i, gda_col // D)),
            pl.BlockSpec((tm, D), lambda i: (i, gna_col // D)),
            pl.BlockSpec((tm, D), lambda i: (i, 0)),
            const(wbd.shape), const(wbn.shape), const(wo.shape), const(g2.shape),
        ],
        out_specs=[pl.BlockSpec((tm, D), lambda i: (i, 0)),
                   pl.BlockSpec((tm, D), lambda i: (i, 0))],
        out_shape=[jax.ShapeDtypeStruct((T, D), F32), jax.ShapeDtypeStruct((T, D), BF16)],
        compiler_params=_cparams(("parallel",)),
        name="mix_out",
    )(o_da, o_na, proj, proj, h0, wbd, wbn, wo, g2)


def _cand_pairs(k):
    return [(i, j) for i in range(k) for j in range(k) if (i + 1) * (j + 1) <= k]


def _topk_cols(s, k, exact_ties):
    n = s.shape[0]
    rows = lax.broadcasted_iota(jnp.int32, s.shape, 0).astype(F32)
    rank = jnp.full(s.shape, float(k), F32)
    vals = []
    for r in range(k):
        m = jnp.max(s, axis=0, keepdims=True)
        hit = s == m
        if exact_ties:
            first = jnp.min(jnp.where(hit, rows, float(n)), axis=0, keepdims=True)
            hit = rows == first
        rank = jnp.where(hit, float(r), rank)
        s = jnp.where(hit, -jnp.inf, s)
        vals.append(m)
    if exact_ties:
        return vals, rank, jnp.ones((1, s.shape[1]), F32)
    want = float(k * (k - 1) // 2 + (n - k) * k)
    ok = (jnp.sum(rank, axis=0, keepdims=True) == want).astype(F32)
    return vals, rank, ok


def _route_head(s1, s2, topk, pairs, exact_ties):
    npad = -(-len(pairs) // 8) * 8
    tm = s1.shape[1]
    v1, rank1, ok1 = _topk_cols(s1, topk, exact_ties)
    v2, rank2, ok2 = _topk_cols(s2, topk, exact_ties)
    cand = jnp.concatenate([v1[i] + v2[j] for i, j in pairs]
                           + [jnp.full((npad - len(pairs), tm), -jnp.inf, F32)], axis=0)
    ex = jnp.exp(cand - (v1[0] + v2[0]))
    _, crank, okc = _topk_cols(cand, topk, exact_ties)
    chosen = (crank < float(topk)).astype(F32)
    z = jnp.sum(chosen * ex, axis=0, keepdims=True)
    jn = jnp.zeros(s1.shape, F32)
    for i in range(topk):
        rows_i = [r for r, (pi, _) in enumerate(pairs) if pi == i]
        cnt = functools.reduce(lambda a, b: a + b, [chosen[r:r + 1] for r in rows_i])
        jn = jnp.where(rank1 == float(i), cnt, jn)
    tables = (rank2, jnp.exp(s2 - v2[0]), jn, jnp.exp(s1 - v1[0]) / z)
    return tables, jnp.min(ok1 * ok2 * okc) > 0.5


def _route_body(xn_ref, wq_ref, sk_ref, r2_ref, e2_ref, jn_ref, e1_ref, q_sc, *, heads, nk, topk):
    q = jnp.dot(xn_ref[...], wq_ref[...], preferred_element_type=F32).astype(BF16)
    for h in range(heads):
        q_sc[h] = q[:, h * 2 * nk:(h + 1) * 2 * nk]
    pairs = _cand_pairs(topk)

    def per_head(h, _):
        qh = q_sc[h]
        s1 = lax.dot_general(sk_ref[0], qh[:, :nk], _NT, preferred_element_type=F32)
        s2 = lax.dot_general(sk_ref[1], qh[:, nk:], _NT, preferred_element_type=F32)

        def write(tables):
            for ref, val in zip((r2_ref, e2_ref, jn_ref, e1_ref), tables):
                ref[h] = val.astype(ref.dtype)

        tables, ok = _route_head(s1, s2, topk, pairs, exact_ties=False)
        write(tables)

        @pl.when(jnp.logical_not(ok))
        def _():
            write(_route_head(s1, s2, topk, pairs, exact_ties=True)[0])

        return 0

    lax.fori_loop(0, heads, per_head, 0)


def _peer_route(xn2, wq, sk, *, heads, nk, topk, tm):
    T, D = xn2.shape
    body = functools.partial(_route_body, heads=heads, nk=nk, topk=topk)
    out_f32 = jax.ShapeDtypeStruct((heads, nk, T), F32)
    out_bf16 = jax.ShapeDtypeStruct((heads, nk, T), BF16)
    ospec = pl.BlockSpec((heads, nk, tm), lambda i: (0, 0, i))
    return pl.pallas_call(
        body,
        grid=(T // tm,),
        in_specs=[pl.BlockSpec((tm, D), lambda i: (i, 0)),
                  pl.BlockSpec(wq.shape, lambda i: (0, 0), pipeline_mode=pl.Buffered(1)),
                  pl.BlockSpec(sk.shape, lambda i: (0, 0, 0))],
        out_specs=[ospec] * 4,
        out_shape=[out_bf16, out_bf16, out_f32, out_f32],
        scratch_shapes=[pltpu.VMEM((heads, tm, 2 * nk), BF16)],
        compiler_params=_cparams(("parallel",)),
        name="peer_route",
    )(xn2, wq, sk)


def _gelu_tanh(x):
    return 0.5 * x * (1.0 + jnp.tanh(math.sqrt(2.0 / math.pi) * (x + 0.044715 * (x * x * x))))


def _peer_body(x_ref, u_ref, vt_ref, r2_ref, e2_ref, jn_ref, e1_ref, h1_ref, o_ref, acc_ref, *,
               heads, nk, nb, cw):
    j = pl.program_id(1)
    x = x_ref[...]
    nchunk = nb * nk // cw

    def scores(c):
        return lax.dot_general(u_ref[c * cw:(c + 1) * cw, :], x, _NT, preferred_element_type=F32)

    a_next = scores(0)
    hts = []
    for c in range(nchunk):
        a = a_next
        if c + 1 < nchunk:
            a_next = scores(c + 1)
        act = _gelu_tanh(a).astype(BF16)
        ws = []
        for nl in range(cw // nk):
            n1 = c * (cw // nk) + nl
            w = None
            for h in range(heads):
                jn = jn_ref[h, n1:n1 + 1, :].astype(BF16)
                e1 = e1_ref[h, n1:n1 + 1, :].astype(BF16)
                t = jnp.where(r2_ref[h] < jn, e2_ref[h] * e1, jnp.zeros((), BF16))
                w = t if w is None else w + t
            ws.append(w)
        hts.append(jnp.concatenate(ws, axis=0) * act)
    total = jnp.dot(vt_ref[...], jnp.concatenate(hts, axis=0), preferred_element_type=F32)

    @pl.when(j == 0)
    def _():
        acc_ref[...] = total

    @pl.when(j > 0)
    def _():
        acc_ref[...] += total

    @pl.when(j == pl.num_programs(1) - 1)
    def _():
        o_ref[...] = h1_ref[...] + acc_ref[...].T


def _peer_experts(xn2, u_bf, vt_bf, r2, e2, jn, e1, h1, *, heads, nk, nb, tm):
    T, D = xn2.shape
    E = u_bf.shape[0]
    eb = nb * nk
    body = functools.partial(_peer_body, heads=heads, nk=nk, nb=nb, cw=MXU_DIM)
    key_spec = pl.BlockSpec((heads, nk, tm), lambda i, j: (0, 0, i))
    row_spec = pl.BlockSpec((heads, nb, tm), lambda i, j: (0, j, i))
    return pl.pallas_call(
        body,
        grid=(T // tm, E // eb),
        in_specs=[pl.BlockSpec((tm, D), lambda i, j: (i, 0)),
                  pl.BlockSpec((eb, D), lambda i, j: (j, 0)),
                  pl.BlockSpec((D, eb), lambda i, j: (0, j)),
                  key_spec, key_spec, row_spec, row_spec,
                  pl.BlockSpec((tm, D), lambda i, j: (i, 0))],
        out_specs=pl.BlockSpec((tm, D), lambda i, j: (i, 0)),
        out_shape=jax.ShapeDtypeStruct((T, D), F32),
        scratch_shapes=[pltpu.VMEM((D, tm), F32)],
        compiler_params=_cparams(("parallel", "arbitrary")),
        name="peer_experts",
    )(xn2, u_bf, vt_bf, r2, e2, jn, e1, h1)


def kernel(x, meta_tokens, norm1_g, w_in, da_q_norm_g, da_k_norm_g, lambda_q1, lambda_k1,
           lambda_q2, lambda_k2, da_out_norm_g, na_q_norm_g, na_k_norm_g, na_rpb,
           w_branch_da, w_branch_na, w_out, norm2_g, peer_w_query, peer_sub_keys, peer_u, peer_v):
    B, S, D = x.shape
    n_meta = meta_tokens.shape[0]
    da_hd = da_q_norm_g.shape[1]
    da_vw = w_branch_da.shape[1]
    da_heads = da_vw // (2 * da_hd)
    da_qkw = da_heads * 2 * da_hd
    na_hd = na_q_norm_g.shape[1]
    na_w = w_branch_na.shape[1]
    na_heads = na_rpb.shape[1]
    nk = peer_sub_keys.shape[2]
    peer_heads = peer_w_query.shape[2] // (2 * nk)
    splits = (da_qkw, da_qkw, da_vw, na_w, na_w, na_w, D, D)
    col = np.concatenate([[0], np.cumsum(splits)]).tolist()
    assert w_in.shape[0] == 1
    assert w_in.shape[2] == col[-1] and S % ROW_BLOCK == 0 and S % GRID_W == 0
    assert da_hd == na_hd and 2 * da_hd == LANES and nk == LANES and n_meta <= META_ROWS
    l = 0

    T = B * S
    tn = 512
    tm_proj = max(t for t in (1024, 512, 256) if T % t == 0)
    tm_peer = 512 if T % 512 == 0 else 256

    xr = x.reshape(T, D)
    xm = jnp.pad(meta_tokens.astype(x.dtype), ((0, META_ROWS - n_meta), (0, 0)))
    log2e = math.log2(math.e)
    slopes = jnp.asarray(2.0 ** (-8.0 * np.arange(1, da_heads + 1) / da_heads) * log2e, dtype=F32)
    ones = lambda n: jnp.ones((n,), F32)

    gcols = jnp.concatenate([
        jnp.tile(da_q_norm_g[l].astype(F32), da_qkw // da_hd) * (da_hd ** -0.5 * log2e),
        jnp.tile(da_k_norm_g[l].astype(F32), da_qkw // da_hd),
        ones(da_vw),
        jnp.tile(na_q_norm_g[l].astype(F32), na_heads) * na_hd ** -0.5,
        jnp.tile(na_k_norm_g[l].astype(F32), na_heads),
        ones(na_w + 2 * D)])[None]
    norm_ranges = ((col[0] // tn, col[2] // tn), (col[3] // tn, col[5] // tn))
    proj_args = dict(norm_ranges=norm_ranges, sig_start=col[6] // tn, head_dim=da_hd, tn=tn)
    g1 = norm1_g[l][None].astype(F32)
    w_in_bf = w_in[l].astype(BF16)
    proj = _inproj(xr, g1, w_in_bf, gcols, tm=tm_proj, **proj_args)
    projm = _inproj(xm, g1, w_in_bf, gcols, tm=META_ROWS, **proj_args)

    lam_init = 0.8 - 0.6 * math.exp(-0.3 * l)
    lamv = jnp.stack([lambda_q1[l], lambda_k1[l], lambda_q2[l], lambda_k2[l]]).astype(F32)
    o_da = _diff_attention(proj, projm, slopes, lamv, da_out_norm_g[l][None].astype(F32),
                           B=B, S=S, n_meta=n_meta, H=da_heads, hd=da_hd,
                           q_col=col[0], k_col=col[1], v_col=col[2], lam_init=lam_init,
                           tq=ROW_BLOCK, nch=8)

    qrows = ROW_BLOCK // GRID_W
    bias_tab = _na_bias_table(na_rpb[l], S // GRID_W, qrows)
    o_na = _nbhd_attention(proj, projm, bias_tab, B=B, S=S, n_meta=n_meta, H=na_heads, hd=na_hd,
                           q_col=col[3], k_col=col[4], v_col=col[5], qrows=qrows, nh=4)

    h1, xn2 = _mix(o_da, o_na, proj, xr, w_branch_da[l].astype(BF16),
                   w_branch_na[l].astype(BF16), w_out[l].astype(BF16),
                   norm2_g[l][None].astype(F32), gda_col=col[6], gna_col=col[7], tm=ROW_BLOCK)

    r2, e2, jn, e1 = _peer_route(xn2, peer_w_query[l].astype(BF16),
                                 peer_sub_keys[l].astype(BF16), heads=peer_heads, nk=nk,
                                 topk=PEER_TOPK, tm=ROW_BLOCK)
    out = _peer_experts(xn2, peer_u[l].astype(BF16), peer_v[l].T.astype(BF16), r2, e2, jn, e1, h1,
                        heads=peer_heads, nk=nk, nb=8, tm=tm_peer)
    return out.reshape(B, S, D)
```

```python
import functools
import math

import numpy as np
import jax
import jax.numpy as jnp
from jax import lax
from jax.experimental import pallas as pl
from jax.experimental.pallas import tpu as pltpu

F32 = jnp.float32
BF16 = jnp.bfloat16

GRID_W = 64
NA_WIN_H = 8
NA_WIN_W = 16
PEER_TOPK = 16
EPS = 1e-6
NEG = -1e30

LANES = 128
MXU_DIM = 256
VMEM_LIMIT = 56 * 1024 * 1024

ROW_BLOCK = 256
META_ROWS = LANES

_NT = (((1,), (1,)), ((), ()))


def _cparams(sem):
    return pltpu.CompilerParams(dimension_semantics=sem, vmem_limit_bytes=VMEM_LIMIT)


def _inproj_body(x_ref, g1_ref, w_ref, gc_ref, o_ref, xn_ref, *, norm_ranges, sig_start, head_dim):
    j = pl.program_id(1)

    @pl.when(j == 0)
    def _():
        x = x_ref[...]
        ms = jnp.mean(x * x, axis=-1, keepdims=True)
        xn_ref[...] = (x * lax.rsqrt(ms + EPS) * g1_ref[...]).astype(BF16)

    tn = o_ref.shape[1]
    chunks = [slice(t * MXU_DIM, (t + 1) * MXU_DIM) for t in range(tn // MXU_DIM)]

    def cols(sl):
        return jnp.dot(xn_ref[...], w_ref[:, sl], preferred_element_type=F32)

    is_norm = functools.reduce(jnp.logical_or, [(j >= a) & (j < b) for a, b in norm_ranges])
    is_sig = j >= sig_start

    @pl.when(is_norm)
    def _():
        r = lax.broadcasted_iota(jnp.int32, (MXU_DIM, MXU_DIM), 0) // head_dim
        c = lax.broadcasted_iota(jnp.int32, (MXU_DIM, MXU_DIM), 1) // head_dim
        ones_bd = (r == c).astype(BF16)
        y = jnp.dot(xn_ref[...], w_ref[...], preferred_element_type=F32)
        yy = (y * y).astype(BF16)
        ss = jnp.concatenate([jnp.dot(yy[:, sl], ones_bd, preferred_element_type=F32)
                              for sl in chunks], axis=1)
        o_ref[...] = (y * lax.rsqrt(ss * (1.0 / head_dim) + EPS) * gc_ref[...]).astype(o_ref.dtype)

    @pl.when(is_sig)
    def _():
        for sl in chunks:
            o_ref[:, sl] = jax.nn.sigmoid(cols(sl)).astype(o_ref.dtype)

    @pl.when(jnp.logical_not(is_norm | is_sig))
    def _():
        for sl in chunks:
            o_ref[:, sl] = cols(sl).astype(o_ref.dtype)


def _inproj(x2d, g1, w_bf, gcols, *, norm_ranges, sig_start, head_dim, tm, tn):
    T, D = x2d.shape
    C = w_bf.shape[1]
    body = functools.partial(_inproj_body, norm_ranges=norm_ranges, sig_start=sig_start,
                             head_dim=head_dim)
    return pl.pallas_call(
        body,
        grid=(T // tm, C // tn),
        in_specs=[
            pl.BlockSpec((tm, D), lambda i, j: (i, 0)),
            pl.BlockSpec((1, D), lambda i, j: (0, 0)),
            pl.BlockSpec((D, tn), lambda i, j: (0, j)),
            pl.BlockSpec((1, tn), lambda i, j: (0, j)),
        ],
        out_specs=pl.BlockSpec((tm, tn), lambda i, j: (i, j)),
        out_shape=jax.ShapeDtypeStruct((T, C), BF16),
        scratch_shapes=[pltpu.VMEM((tm, D), BF16)],
        compiler_params=_cparams(("parallel", "arbitrary")),
        name="inproj",
    )(x2d, g1, w_bf, gcols)


_VAR_BEFORE, _VAR_AFTER, _VAR_PLAIN, _VAR_MASK = 0, 1, 2, 3
_MASK_SCORE = 32768.0


def _alibi_position_tables(slopes, S, n_meta, tkm, hd):
    def top16(x):
        bits = lax.bitcast_convert_type(x, jnp.uint32) & jnp.uint32(0xFFFF0000)
        return lax.bitcast_convert_type(bits, F32)

    def split3(x):
        hi = top16(x)
        mid = top16(x - hi)
        lo = top16(x - hi - mid)
        return [hi, mid, lo]

    def place(cols):
        six = jnp.stack(cols, axis=-1)
        gap = jnp.zeros(six.shape[:-1] + (hd - len(cols),), F32)
        return jnp.concatenate([six, gap, six, gap], axis=-1).astype(BF16)

    qpos = jnp.arange(S, dtype=F32) + n_meta
    kpos = jnp.concatenate([qpos, jnp.arange(tkm, dtype=F32)])
    a = -slopes[:, None] * qpos[None, :]
    b = slopes[:, None] * kpos[None, :]
    qtab = place(split3(a) + [jnp.ones_like(a)] * 3)
    ktab = place([jnp.ones_like(b)] * 3 + split3(b))
    return qtab, ktab


def _da_body(q_ref, k_ref, v_ref, km_ref, vm_ref, qt_ref, kt_ref, own_ref, lamv_ref, og_ref, o_ref,
             kp_ref, vx_ref, qv_ref, sa_ref, sb_ref, *, n_meta, tq, nch, hd, lam_init):
    i = pl.program_id(2)
    S = k_ref.shape[0]
    tkm = km_ref.shape[0]

    @pl.when(i == 0)
    def _():
        for rows, ksrc, vsrc in ((pl.ds(0, S), k_ref, v_ref), (pl.ds(S, tkm), km_ref, vm_ref)):
            k = ksrc[...]
            kt = kt_ref[0, rows, :]
            lane = lax.broadcasted_iota(jnp.int32, k.shape, 1)
            kp_ref[0, rows, :] = jnp.where(lane < hd, k, kt)
            kp_ref[1, rows, :] = jnp.where(lane >= hd, k, kt)
            vx_ref[rows, :LANES] = vsrc[...]
            vx_ref[rows, LANES:] = jnp.where(lane == 0, 1.0, 0.0).astype(BF16)

    q = q_ref[...]
    qt = qt_ref[0]
    lane = lax.broadcasted_iota(jnp.int32, (tq, LANES), 1)
    mask_lanes = jnp.where((lane == 0) | (lane == hd), -_MASK_SCORE, 0.0).astype(BF16)
    for c in range(2):
        keep = (lane < hd) if c == 0 else (lane >= hd)
        qv_ref[c, _VAR_BEFORE] = jnp.where(keep, q, qt)
        qv_ref[c, _VAR_AFTER] = jnp.where(keep, q, -qt)
        qv_ref[c, _VAR_PLAIN] = jnp.where(keep, q, jnp.zeros_like(q))
        qv_ref[c, _VAR_MASK] = jnp.where(keep, jnp.zeros_like(q), mask_lanes)
    own_bias = own_ref[0]

    def update(carry_c, s_list, vx):
        m, accx = carry_c
        tiles = [s[:, t * LANES:(t + 1) * LANES] for s in s_list for t in range(s.shape[1] // LANES)]
        m_new = jnp.maximum(m, jnp.max(functools.reduce(jnp.maximum, tiles), axis=-1, keepdims=True))
        pv = None
        row = 0
        for s in s_list:
            d = jnp.dot(jnp.exp2(s - m_new).astype(BF16), vx[row:row + s.shape[1]],
                        preferred_element_type=F32)
            pv = d if pv is None else pv + d
            row += s.shape[1]
        return m_new, jnp.exp2(m - m_new) * accx + pv

    def scores(jb, dst_ref):
        for c in range(2):
            for kk in range(nch):
                g = jb * nch + kk
                var = jnp.where(g < i, _VAR_BEFORE, jnp.where(g == i, _VAR_MASK, _VAR_AFTER))
                kc = kp_ref[c, pl.ds(pl.multiple_of(g * tq, tq), tq), :]
                dst_ref[c, kk] = lax.dot_general(qv_ref[c, var], kc, _NT,
                                                 preferred_element_type=F32)

    def absorb(jb, src_ref, carry):
        vx = vx_ref[pl.ds(pl.multiple_of(jb * (nch * tq), nch * tq), nch * tq), :]
        return tuple(update(carry[c], [src_ref[c, kk] for kk in range(nch)], vx) for c in range(2))

    nsb = S // (nch * tq)
    dv = v_ref.shape[1]
    carry = tuple((jnp.full((tq, 1), NEG, F32), jnp.zeros((tq, 2 * dv), F32)) for _ in range(2))
    scores(0, sa_ref)

    def pair(t, carry):
        scores(2 * t + 1, sb_ref)
        carry = absorb(2 * t, sa_ref, carry)
        scores(2 * t + 2, sa_ref)
        return absorb(2 * t + 1, sb_ref, carry)

    carry = lax.fori_loop(0, nsb // 2 - 1, pair, carry)
    scores(nsb - 1, sb_ref)
    carry = absorb(nsb - 2, sa_ref, carry)

    meta_mask = jnp.where(lax.broadcasted_iota(jnp.int32, (tq, tkm), 1) < n_meta, 0.0, NEG)
    own_rows = pl.ds(pl.multiple_of(i * tq, tq), tq)
    tail_s = []
    for c in range(2):
        s_own = lax.dot_general(qv_ref[c, _VAR_PLAIN], kp_ref[c, own_rows, :], _NT,
                                preferred_element_type=F32) + own_bias
        s_meta = lax.dot_general(qv_ref[c, _VAR_BEFORE], kp_ref[c, pl.ds(S, tkm), :], _NT,
                                 preferred_element_type=F32) + meta_mask
        tail_s.append([s_own, s_meta])
    carry = absorb(nsb - 1, sb_ref, carry)
    vx_tail = jnp.concatenate([vx_ref[own_rows, :], vx_ref[pl.ds(S, tkm), :]], axis=0)
    carry = tuple(update(carry[c], tail_s[c], vx_tail) for c in range(2))

    lv = lamv_ref[...]
    lam = (jnp.exp(jnp.sum(lv[0:1] * lv[1:2], axis=-1, keepdims=True))
           - jnp.exp(jnp.sum(lv[2:3] * lv[3:4], axis=-1, keepdims=True)) + lam_init)
    (_, ax0), (_, ax1) = carry
    o = ax0[:, :dv] / ax0[:, dv:dv + 1] - lam * (ax1[:, :dv] / ax1[:, dv:dv + 1])
    o = o * lax.rsqrt(jnp.mean(o * o, axis=-1, keepdims=True) + EPS) * og_ref[...] * (1.0 - lam_init)
    o_ref[...] = o.astype(o_ref.dtype)


def _diff_attention(proj, projm, slopes, lamv, og, *, B, S, n_meta, H, hd, q_col, k_col, v_col,
                    lam_init, tq, nch):
    dv = 2 * hd
    nq = S // tq
    assert S % (2 * nch * tq) == 0 and dv == LANES
    body = functools.partial(_da_body, n_meta=n_meta, tq=tq, nch=nch, hd=hd, lam_init=lam_init)
    qb, kb, vb = q_col // dv, k_col // dv, v_col // dv
    tkm = projm.shape[0]
    qtab, ktab = _alibi_position_tables(slopes, S, n_meta, tkm, hd)
    idx = jnp.arange(tq, dtype=F32)
    own = -slopes[:, None, None] * jnp.abs(idx[:, None] - idx[None, :])[None]
    return pl.pallas_call(
        body,
        grid=(B, H, nq),
        in_specs=[
            pl.BlockSpec((tq, dv), lambda b, h, i: (b * nq + i, qb + h)),
            pl.BlockSpec((S, dv), lambda b, h, i: (b, kb + h)),
            pl.BlockSpec((S, dv), lambda b, h, i: (b, vb + h)),
            pl.BlockSpec((tkm, dv), lambda b, h, i: (0, kb + h)),
            pl.BlockSpec((tkm, dv), lambda b, h, i: (0, vb + h)),
            pl.BlockSpec((1, tq, dv), lambda b, h, i: (h, i, 0)),
            pl.BlockSpec((1, S + tkm, dv), lambda b, h, i: (h, 0, 0)),
            pl.BlockSpec((1, tq, tq), lambda b, h, i: (h, 0, 0)),
            pl.BlockSpec(lamv.shape, lambda b, h, i: (0, 0)),
            pl.BlockSpec((1, dv), lambda b, h, i: (0, 0)),
        ],
        out_specs=pl.BlockSpec((tq, dv), lambda b, h, i: (b * nq + i, h)),
        out_shape=jax.ShapeDtypeStruct((B * S, H * dv), BF16),
        scratch_shapes=[pltpu.VMEM((2, S + tkm, dv), BF16),
                        pltpu.VMEM((S + tkm, 2 * dv), BF16),
                        pltpu.VMEM((2, 4, tq, dv), BF16),
                        pltpu.VMEM((2, nch, tq, tq), F32),
                        pltpu.VMEM((2, nch, tq, tq), F32)],
        compiler_params=_cparams(("parallel", "parallel", "arbitrary")),
        name="diff_attention",
    )(proj, proj, proj, projm, projm, qtab, ktab, own, lamv, og)


def _na_bias_table(rpb, rows, qrows):
    W = GRID_W
    H, n_dr, n_dc = rpb.shape
    n_blk = rows // qrows
    kh = min(NA_WIN_H, rows)
    qc = np.arange(W)[:, None]
    kc = np.arange(W)[None, :]
    cs = np.clip(qc - NA_WIN_W // 2, 0, W - NA_WIN_W)
    col_ok = (kc >= cs) & (kc < cs + NA_WIN_W)
    onehot = ((kc - qc + (NA_WIN_W - 1))[None] == np.arange(n_dc)[:, None, None]) & col_ok[None]
    tiles = jnp.einsum('hrd,dqk->hrqk', rpb.astype(F32), jnp.asarray(onehot, F32),
                       precision=lax.Precision.HIGHEST)
    tiles = jnp.where(jnp.asarray(col_ok), tiles, NEG)
    tiles = jnp.concatenate([tiles, jnp.full((H, 1, W, W), NEG, F32)], axis=1)
    sel = np.full((3, qrows, 3 * qrows), n_dr, np.int32)
    for v, (qb, kb0) in enumerate(((0, 0), (1, 0), (n_blk - 1, n_blk - 3))):
        r = (qb * qrows + np.arange(qrows))[:, None]
        kr = (kb0 * qrows + np.arange(3 * qrows))[None, :]
        r0 = np.clip(r - kh // 2, 0, rows - kh)
        row_ok = (kr >= r0) & (kr < r0 + kh)
        sel[v] = np.where(row_ok, kr - r + (NA_WIN_H - 1), n_dr)
    g = jnp.take(tiles, jnp.asarray(sel.reshape(-1)), axis=1)
    g = g.reshape(H, 3, qrows, 3 * qrows, W, W).transpose(1, 0, 2, 4, 3, 5)
    return g.reshape(3, H, qrows * W, 3 * qrows * W)


def _na_body(q_ref, k0_ref, k1_ref, k2_ref, v0_ref, v1_ref, v2_ref, km_ref, vm_ref, bias_ref,
             o_ref, *, hd, n_meta, nh):
    q = q_ref[...]
    k = jnp.concatenate([k0_ref[...], k1_ref[...], k2_ref[...]], axis=0)
    v = jnp.concatenate([v0_ref[...], v1_ref[...], v2_ref[...]], axis=0)
    km = km_ref[...]
    vm = vm_ref[...]
    lane = lax.broadcasted_iota(jnp.int32, q.shape, 1)
    zero = jnp.zeros_like(q)
    midx = lax.broadcasted_iota(jnp.int32, (1, km.shape[0]), 1)
    mmask = jnp.where(midx < n_meta, 0.0, NEG)
    out = jnp.zeros(q.shape, F32)
    for hh in range(nh):
        mine = (lane >= hh * hd) & (lane < (hh + 1) * hd)
        qh = jnp.where(mine, q, zero)
        s = lax.dot_general(qh, k, _NT, preferred_element_type=F32) + bias_ref[0, hh]
        sm = lax.dot_general(qh, km, _NT, preferred_element_type=F32) + mmask
        m = jnp.maximum(jnp.max(s, axis=-1, keepdims=True), jnp.max(sm, axis=-1, keepdims=True))
        p = jnp.exp(s - m)
        pm = jnp.exp(sm - m)
        l = jnp.sum(p, axis=-1, keepdims=True) + jnp.sum(pm, axis=-1, keepdims=True)
        o = (jnp.dot(p.astype(BF16), v, preferred_element_type=F32)
             + jnp.dot(pm.astype(BF16), vm, preferred_element_type=F32))
        out = jnp.where(mine, o / l, out)
    o_ref[...] = out.astype(o_ref.dtype)


def _nbhd_attention(proj, projm, bias_tab, *, B, S, n_meta, H, hd, q_col, k_col, v_col, qrows, nh):
    Q = qrows * GRID_W
    nqb = S // Q
    wb = nh * hd
    qc, kc, vc = q_col // wb, k_col // wb, v_col // wb
    mrows = projm.shape[0]

    def kb0(i):
        return jnp.clip(i - 1, 0, nqb - 3)

    def variant(i):
        return jnp.where(i == 0, 0, jnp.where(i == nqb - 1, 2, 1))

    def kv_spec(col, d):
        return pl.BlockSpec((Q, wb), lambda hg, b, i: (b * nqb + kb0(i) + d, col + hg))

    def meta_spec(col):
        return pl.BlockSpec((mrows, wb), lambda hg, b, i: (0, col + hg))

    body = functools.partial(_na_body, hd=hd, n_meta=n_meta, nh=nh)
    return pl.pallas_call(
        body,
        grid=(H // nh, B, nqb),
        in_specs=[pl.BlockSpec((Q, wb), lambda hg, b, i: (b * nqb + i, qc + hg)),
                  kv_spec(kc, 0), kv_spec(kc, 1), kv_spec(kc, 2),
                  kv_spec(vc, 0), kv_spec(vc, 1), kv_spec(vc, 2),
                  meta_spec(kc), meta_spec(vc),
                  pl.BlockSpec((1, nh, Q, 3 * Q), lambda hg, b, i: (variant(i), hg, 0, 0))],
        out_specs=pl.BlockSpec((Q, wb), lambda hg, b, i: (b * nqb + i, hg)),
        out_shape=jax.ShapeDtypeStruct((B * S, H * hd), BF16),
        compiler_params=_cparams(("parallel", "parallel", "arbitrary")),
        name="nbhd_attention",
    )(proj, proj, proj, proj, proj, proj, proj, projm, projm, bias_tab)


def _mix_body(oda_ref, ona_ref, gda_ref, gna_ref, h_ref, wbd_ref, wbn_ref, wo_ref, g2_ref,
              h1_ref, xn2_ref):
    yda = jnp.dot(oda_ref[...], wbd_ref[...], preferred_element_type=F32)
    yna = jnp.dot(ona_ref[...], wbn_ref[...], preferred_element_type=F32)
    mixed = gda_ref[...].astype(F32) * yda + gna_ref[...].astype(F32) * yna
    h1 = h_ref[...] + jnp.dot(mixed.astype(BF16), wo_ref[...], preferred_element_type=F32)
    h1_ref[...] = h1
    ms = jnp.mean(h1 * h1, axis=-1, keepdims=True)
    xn2_ref[...] = (h1 * lax.rsqrt(ms + EPS) * g2_ref[...]).astype(xn2_ref.dtype)


def _mix(o_da, o_na, proj, h0, wbd, wbn, wo, g2, *, gda_col, gna_col, tm):
    T, D = h0.shape
    const = lambda shape: pl.BlockSpec(shape, lambda i: (0, 0), pipeline_mode=pl.Buffered(1))
    return pl.pallas_call(
        _mix_body,
        grid=(T // tm,),
        in_specs=[
            pl.BlockSpec((tm, o_da.shape[1]), lambda i: (i, 0)),
            pl.BlockSpec((tm, o_na.shape[1]), lambda i: (i, 0)),
            pl.BlockSpec((tm, D), lambda i: (i, gda_col // D)),
            pl.BlockSpec((tm, D), lambda i: (i, gna_col // D)),
            pl.BlockSpec((tm, D), lambda i: (i, 0)),
            const(wbd.shape), const(wbn.shape), const(wo.shape), const(g2.shape),
        ],
        out_specs=[pl.BlockSpec((tm, D), lambda i: (i, 0)),
                   pl.BlockSpec((tm, D), lambda i: (i, 0))],
        out_shape=[jax.ShapeDtypeStruct((T, D), F32), jax.ShapeDtypeStruct((T, D), BF16)],
        compiler_params=_cparams(("parallel",)),
        name="mix_out",
    )(o_da, o_na, proj, proj, h0, wbd, wbn, wo, g2)


def _cand_pairs(k):
    return [(i, j) for i in range(k) for j in range(k) if (i + 1) * (j + 1) <= k]


def _topk_cols(s, k, exact_ties):
    n = s.shape[0]
    rows = lax.broadcasted_iota(jnp.int32, s.shape, 0).astype(F32)
    rank = jnp.full(s.shape, float(k), F32)
    vals = []
    for r in range(k):
        m = jnp.max(s, axis=0, keepdims=True)
        hit = s == m
        if exact_ties:
            first = jnp.min(jnp.where(hit, rows, float(n)), axis=0, keepdims=True)
            hit = rows == first
        rank = jnp.where(hit, float(r), rank)
        s = jnp.where(hit, -jnp.inf, s)
        vals.append(m)
    if exact_ties:
        return vals, rank, jnp.ones((1, s.shape[1]), F32)
    want = float(k * (k - 1) // 2 + (n - k) * k)
    ok = (jnp.sum(rank, axis=0, keepdims=True) == want).astype(F32)
    return vals, rank, ok


def _route_head(s1, s2, topk, pairs, exact_ties):
    npad = -(-len(pairs) // 8) * 8
    tm = s1.shape[1]
    v1, rank1, ok1 = _topk_cols(s1, topk, exact_ties)
    v2, rank2, ok2 = _topk_cols(s2, topk, exact_ties)
    cand = jnp.concatenate([v1[i] + v2[j] for i, j in pairs]
                           + [jnp.full((npad - len(pairs), tm), -jnp.inf, F32)], axis=0)
    ex = jnp.exp(cand - (v1[0] + v2[0]))
    _, crank, okc = _topk_cols(cand, topk, exact_ties)
    chosen = (crank < float(topk)).astype(F32)
    z = jnp.sum(chosen * ex, axis=0, keepdims=True)
    jn = jnp.zeros(s1.shape, F32)
    for i in range(topk):
        rows_i = [r for r, (pi, _) in enumerate(pairs) if pi == i]
        cnt = functools.reduce(lambda a, b: a + b, [chosen[r:r + 1] for r in rows_i])
        jn = jnp.where(rank1 == float(i), cnt, jn)
    tables = (rank2, jnp.exp(s2 - v2[0]), jn, jnp.exp(s1 - v1[0]) / z)
    return tables, jnp.min(ok1 * ok2 * okc) > 0.5


def _route_body(xn_ref, wq_ref, sk_ref, r2_ref, e2_ref, jn_ref, e1_ref, q_sc, *, heads, nk, topk):
    q = jnp.dot(xn_ref[...], wq_ref[...], preferred_element_type=F32).astype(BF16)
    for h in range(heads):
        q_sc[h] = q[:, h * 2 * nk:(h + 1) * 2 * nk]
    pairs = _cand_pairs(topk)

    def per_head(h, _):
        qh = q_sc[h]
        s1 = lax.dot_general(sk_ref[0], qh[:, :nk], _NT, preferred_element_type=F32)
        s2 = lax.dot_general(sk_ref[1], qh[:, nk:], _NT, preferred_element_type=F32)

        def write(tables):
            for ref, val in zip((r2_ref, e2_ref, jn_ref, e1_ref), tables):
                ref[h] = val.astype(ref.dtype)

        tables, ok = _route_head(s1, s2, topk, pairs, exact_ties=False)
        write(tables)

        @pl.when(jnp.logical_not(ok))
        def _():
            write(_route_head(s1, s2, topk, pairs, exact_ties=True)[0])

        return 0

    lax.fori_loop(0, heads, per_head, 0)


def _peer_route(xn2, wq, sk, *, heads, nk, topk, tm):
    T, D = xn2.shape
    body = functools.partial(_route_body, heads=heads, nk=nk, topk=topk)
    out_f32 = jax.ShapeDtypeStruct((heads, nk, T), F32)
    out_bf16 = jax.ShapeDtypeStruct((heads, nk, T), BF16)
    ospec = pl.BlockSpec((heads, nk, tm), lambda i: (0, 0, i))
    return pl.pallas_call(
        body,
        grid=(T // tm,),
        in_specs=[pl.BlockSpec((tm, D), lambda i: (i, 0)),
                  pl.BlockSpec(wq.shape, lambda i: (0, 0), pipeline_mode=pl.Buffered(1)),
                  pl.BlockSpec(sk.shape, lambda i: (0, 0, 0))],
        out_specs=[ospec] * 4,
        out_shape=[out_bf16, out_bf16, out_f32, out_f32],
        scratch_shapes=[pltpu.VMEM((heads, tm, 2 * nk), BF16)],
        compiler_params=_cparams(("parallel",)),
        name="peer_route",
    )(xn2, wq, sk)


def _gelu_tanh(x):
    return 0.5 * x * (1.0 + jnp.tanh(math.sqrt(2.0 / math.pi) * (x + 0.044715 * (x * x * x))))


def _peer_body(x_ref, u_ref, vt_ref, r2_ref, e2_ref, jn_ref, e1_ref, h1_ref, o_ref,
               acc_ref, hta_ref, htb_ref, *, heads, nk, nb, cw):
    j = pl.program_id(1)
    n_blocks = pl.num_programs(1) - 1
    nchunk = nb * nk // cw

    def scores(c):
        return lax.dot_general(u_ref[c * cw:(c + 1) * cw, :], x_ref[...], _NT,
                               preferred_element_type=F32)

    def gated(between=lambda c: None):
        a_next = scores(0)
        hts = []
        for c in range(nchunk):
            a = a_next
            if c + 1 < nchunk:
                a_next = scores(c + 1)
            between(c)
            act = _gelu_tanh(a).astype(BF16)
            ws = []
            for nl in range(cw // nk):
                n1 = c * (cw // nk) + nl
                w = None
                for h in range(heads):
                    jn = jn_ref[h, n1:n1 + 1, :].astype(BF16)
                    e1 = e1_ref[h, n1:n1 + 1, :].astype(BF16)
                    t = jnp.where(r2_ref[h] < jn, e2_ref[h] * e1, jnp.zeros((), BF16))
                    w = t if w is None else w + t
                ws.append(w)
            hts.append(jnp.concatenate(ws, axis=0) * act)
        return jnp.concatenate(hts, axis=0)

    slab = acc_ref.shape[0] // nchunk

    def apply_prev(src_ref):
        def slab_update(c):
            rows = slice(c * slab, (c + 1) * slab)
            acc_ref[rows, :] += jnp.dot(vt_ref[rows, :], src_ref[...], preferred_element_type=F32)
        return slab_update

    @pl.when(j == 0)
    def _():
        acc_ref[...] = jnp.zeros_like(acc_ref)
        hta_ref[...] = gated()

    middle = (j > 0) & (j < n_blocks)

    @pl.when(middle & (j % 2 == 1))
    def _():
        htb_ref[...] = gated(between=apply_prev(hta_ref))

    @pl.when(middle & (j % 2 == 0))
    def _():
        hta_ref[...] = gated(between=apply_prev(htb_ref))

    @pl.when(j == n_blocks)
    def _():
        last = apply_prev(htb_ref)
        for c in range(nchunk):
            last(c)
        o_ref[...] = h1_ref[...] + acc_ref[...].T


def _peer_experts(xn2, u_bf, vt_bf, r2, e2, jn, e1, h1, *, heads, nk, nb, tm):
    T, D = xn2.shape
    E = u_bf.shape[0]
    eb = nb * nk
    ne = E // eb
    assert ne % 2 == 0
    body = functools.partial(_peer_body, heads=heads, nk=nk, nb=nb, cw=MXU_DIM)
    cur = lambda j: jnp.minimum(j, ne - 1)
    prev = lambda j: jnp.maximum(j - 1, 0)
    key_spec = pl.BlockSpec((heads, nk, tm), lambda i, j: (0, 0, i))
    row_spec = pl.BlockSpec((heads, nb, tm), lambda i, j: (0, cur(j), i))
    return pl.pallas_call(
        body,
        grid=(T // tm, ne + 1),
        in_specs=[pl.BlockSpec((tm, D), lambda i, j: (i, 0)),
                  pl.BlockSpec((eb, D), lambda i, j: (cur(j), 0)),
                  pl.BlockSpec((D, eb), lambda i, j: (0, prev(j))),
                  key_spec, key_spec, row_spec, row_spec,
                  pl.BlockSpec((tm, D), lambda i, j: (i, 0))],
        out_specs=pl.BlockSpec((tm, D), lambda i, j: (i, 0)),
        out_shape=jax.ShapeDtypeStruct((T, D), F32),
        scratch_shapes=[pltpu.VMEM((D, tm), F32), pltpu.VMEM((eb, tm), BF16),
                        pltpu.VMEM((eb, tm), BF16)],
        compiler_params=_cparams(("parallel", "arbitrary")),
        name="peer_experts",
    )(xn2, u_bf, vt_bf, r2, e2, jn, e1, h1)


def kernel(x, meta_tokens, norm1_g, w_in, da_q_norm_g, da_k_norm_g, lambda_q1, lambda_k1,
           lambda_q2, lambda_k2, da_out_norm_g, na_q_norm_g, na_k_norm_g, na_rpb,
           w_branch_da, w_branch_na, w_out, norm2_g, peer_w_query, peer_sub_keys, peer_u, peer_v):
    B, S, D = x.shape
    n_meta = meta_tokens.shape[0]
    da_hd = da_q_norm_g.shape[1]
    da_vw = w_branch_da.shape[1]
    da_heads = da_vw // (2 * da_hd)
    da_qkw = da_heads * 2 * da_hd
    na_hd = na_q_norm_g.shape[1]
    na_w = w_branch_na.shape[1]
    na_heads = na_rpb.shape[1]
    nk = peer_sub_keys.shape[2]
    peer_heads = peer_w_query.shape[2] // (2 * nk)
    splits = (da_qkw, da_qkw, da_vw, na_w, na_w, na_w, D, D)
    col = np.concatenate([[0], np.cumsum(splits)]).tolist()
    assert w_in.shape[0] == 1
    assert w_in.shape[2] == col[-1] and S % ROW_BLOCK == 0 and S % GRID_W == 0
    assert da_hd == na_hd and 2 * da_hd == LANES and nk == LANES and n_meta <= META_ROWS
    l = 0

    T = B * S
    tn = 512
    tm_proj = max(t for t in (1024, 512, 256) if T % t == 0)
    tm_peer = 512 if T % 512 == 0 else 256

    xr = x.reshape(T, D)
    xm = jnp.pad(meta_tokens.astype(x.dtype), ((0, META_ROWS - n_meta), (0, 0)))
    log2e = math.log2(math.e)
    slopes = jnp.asarray(2.0 ** (-8.0 * np.arange(1, da_heads + 1) / da_heads) * log2e, dtype=F32)
    ones = lambda n: jnp.ones((n,), F32)

    gcols = jnp.concatenate([
        jnp.tile(da_q_norm_g[l].astype(F32), da_qkw // da_hd) * (da_hd ** -0.5 * log2e),
        jnp.tile(da_k_norm_g[l].astype(F32), da_qkw // da_hd),
        ones(da_vw),
        jnp.tile(na_q_norm_g[l].astype(F32), na_heads) * na_hd ** -0.5,
        jnp.tile(na_k_norm_g[l].astype(F32), na_heads),
        ones(na_w + 2 * D)])[None]
    norm_ranges = ((col[0] // tn, col[2] // tn), (col[3] // tn, col[5] // tn))
    proj_args = dict(norm_ranges=norm_ranges, sig_start=col[6] // tn, head_dim=da_hd, tn=tn)
    g1 = norm1_g[l][None].astype(F32)
    w_in_bf = w_in[l].astype(BF16)
    proj = _inproj(xr, g1, w_in_bf, gcols, tm=tm_proj, **proj_args)
    projm = _inproj(xm, g1, w_in_bf, gcols, tm=META_ROWS, **proj_args)

    lam_init = 0.8 - 0.6 * math.exp(-0.3 * l)
    lamv = jnp.stack([lambda_q1[l], lambda_k1[l], lambda_q2[l], lambda_k2[l]]).astype(F32)
    o_da = _diff_attention(proj, projm, slopes, lamv, da_out_norm_g[l][None].astype(F32),
                           B=B, S=S, n_meta=n_meta, H=da_heads, hd=da_hd,
                           q_col=col[0], k_col=col[1], v_col=col[2], lam_init=lam_init,
                           tq=ROW_BLOCK, nch=8)

    qrows = ROW_BLOCK // GRID_W
    bias_tab = _na_bias_table(na_rpb[l], S // GRID_W, qrows)
    o_na = _nbhd_attention(proj, projm, bias_tab, B=B, S=S, n_meta=n_meta, H=na_heads, hd=na_hd,
                           q_col=col[3], k_col=col[4], v_col=col[5], qrows=qrows, nh=4)

    h1, xn2 = _mix(o_da, o_na, proj, xr, w_branch_da[l].astype(BF16),
                   w_branch_na[l].astype(BF16), w_out[l].astype(BF16),
                   norm2_g[l][None].astype(F32), gda_col=col[6], gna_col=col[7], tm=ROW_BLOCK)

    r2, e2, jn, e1 = _peer_route(xn2, peer_w_query[l].astype(BF16),
                                 peer_sub_keys[l].astype(BF16), heads=peer_heads, nk=nk,
                                 topk=PEER_TOPK, tm=ROW_BLOCK)
    out = _peer_experts(xn2, peer_u[l].astype(BF16), peer_v[l].T.astype(BF16), r2, e2, jn, e1, h1,
                        heads=peer_heads, nk=nk, nb=8, tm=tm_peer)
    return out.reshape(B, S, D)
```

```python
import functools
import math

import numpy as np
import jax
import jax.numpy as jnp
from jax import lax
from jax.experimental import pallas as pl
from jax.experimental.pallas import tpu as pltpu

F32 = jnp.float32
BF16 = jnp.bfloat16

GRID_W = 64
NA_WIN_H = 8
NA_WIN_W = 16
PEER_TOPK = 16
EPS = 1e-6
NEG = -1e30

LANES = 128
MXU_DIM = 256
VMEM_LIMIT = 56 * 1024 * 1024

ROW_BLOCK = 256
META_ROWS = LANES

_NT = (((1,), (1,)), ((), ()))


def _cparams(sem):
    return pltpu.CompilerParams(dimension_semantics=sem, vmem_limit_bytes=VMEM_LIMIT)


def _inproj_body(x_ref, g1_ref, w_ref, gc_ref, o_ref, xn_ref, *, norm_ranges, sig_start, head_dim):
    j = pl.program_id(1)

    @pl.when(j == 0)
    def _():
        x = x_ref[...]
        ms = jnp.mean(x * x, axis=-1, keepdims=True)
        xn_ref[...] = (x * lax.rsqrt(ms + EPS) * g1_ref[...]).astype(BF16)

    tn = o_ref.shape[1]
    chunks = [slice(t * MXU_DIM, (t + 1) * MXU_DIM) for t in range(tn // MXU_DIM)]

    def cols(sl):
        return jnp.dot(xn_ref[...], w_ref[:, sl], preferred_element_type=F32)

    is_norm = functools.reduce(jnp.logical_or, [(j >= a) & (j < b) for a, b in norm_ranges])
    is_sig = j >= sig_start

    @pl.when(is_norm)
    def _():
        r = lax.broadcasted_iota(jnp.int32, (MXU_DIM, MXU_DIM), 0) // head_dim
        c = lax.broadcasted_iota(jnp.int32, (MXU_DIM, MXU_DIM), 1) // head_dim
        ones_bd = (r == c).astype(BF16)
        y = jnp.dot(xn_ref[...], w_ref[...], preferred_element_type=F32)
        yy = (y * y).astype(BF16)
        ss = jnp.concatenate([jnp.dot(yy[:, sl], ones_bd, preferred_element_type=F32)
                              for sl in chunks], axis=1)
        o_ref[...] = (y * lax.rsqrt(ss * (1.0 / head_dim) + EPS) * gc_ref[...]).astype(o_ref.dtype)

    @pl.when(is_sig)
    def _():
        for sl in chunks:
            o_ref[:, sl] = jax.nn.sigmoid(cols(sl)).astype(o_ref.dtype)

    @pl.when(jnp.logical_not(is_norm | is_sig))
    def _():
        for sl in chunks:
            o_ref[:, sl] = cols(sl).astype(o_ref.dtype)


def _inproj(x2d, g1, w_bf, gcols, *, norm_ranges, sig_start, head_dim, tm, tn):
    T, D = x2d.shape
    C = w_bf.shape[1]
    body = functools.partial(_inproj_body, norm_ranges=norm_ranges, sig_start=sig_start,
                             head_dim=head_dim)
    return pl.pallas_call(
        body,
        grid=(T // tm, C // tn),
        in_specs=[
            pl.BlockSpec((tm, D), lambda i, j: (i, 0)),
            pl.BlockSpec((1, D), lambda i, j: (0, 0)),
            pl.BlockSpec((D, tn), lambda i, j: (0, j)),
            pl.BlockSpec((1, tn), lambda i, j: (0, j)),
        ],
        out_specs=pl.BlockSpec((tm, tn), lambda i, j: (i, j)),
        out_shape=jax.ShapeDtypeStruct((T, C), BF16),
        scratch_shapes=[pltpu.VMEM((tm, D), BF16)],
        compiler_params=_cparams(("parallel", "arbitrary")),
        name="inproj",
    )(x2d, g1, w_bf, gcols)


_VAR_BEFORE, _VAR_AFTER, _VAR_PLAIN, _VAR_MASK = 0, 1, 2, 3
_MASK_SCORE = 32768.0


def _split3(x):
    hi = x.astype(BF16).astype(F32)
    mid = (x - hi).astype(BF16).astype(F32)
    lo = (x - hi - mid).astype(BF16).astype(F32)
    return hi, mid, lo


def _extras(lane, base, first3, last3):
    out = jnp.zeros(lane.shape, F32)
    for t, val in enumerate(tuple(first3) + tuple(last3)):
        out = jnp.where(lane == base + t, val, out)
    return out


def _da_body(slopes_ref, q_ref, k_ref, v_ref, km_ref, vm_ref, lamv_ref, og_ref, o_ref,
             kp_ref, vx_ref, qv_ref, sa_ref, sb_ref, *, n_meta, tq, nch, hd, lam_init):
    h = pl.program_id(1)
    i = pl.program_id(2)
    slope = slopes_ref[h]
    S = k_ref.shape[0]
    tkm = km_ref.shape[0]
    one = jnp.ones((), F32)
    zero = jnp.zeros((), F32)

    @pl.when(i == 0)
    def _():
        for rows, ksrc, vsrc, pos0 in ((pl.ds(0, S), k_ref, v_ref, n_meta),
                                       (pl.ds(S, tkm), km_ref, vm_ref, 0)):
            k = ksrc[...]
            lane = lax.broadcasted_iota(jnp.int32, k.shape, 1)
            kpos = (lax.broadcasted_iota(jnp.int32, k.shape, 0) + pos0).astype(F32)
            b3 = _split3(slope * kpos)
            kp_ref[0, rows, :] = jnp.where(lane < hd, k, _extras(lane, hd, (one,) * 3, b3).astype(BF16))
            kp_ref[1, rows, :] = jnp.where(lane >= hd, k, _extras(lane, 0, (one,) * 3, b3).astype(BF16))
            vx_ref[rows, :LANES] = vsrc[...]
            vx_ref[rows, LANES:] = jnp.where(lane == 0, 1.0, 0.0).astype(BF16)

    q = q_ref[...]
    lane = lax.broadcasted_iota(jnp.int32, (tq, LANES), 1)
    qpos = (i * tq + n_meta + lax.broadcasted_iota(jnp.int32, (tq, LANES), 0)).astype(F32)
    a3 = _split3(-slope * qpos)
    for c, base in ((0, hd), (1, 0)):
        keep = (lane < hd) if c == 0 else (lane >= hd)
        ext = _extras(lane, base, a3, (one,) * 3)
        qv_ref[c, _VAR_BEFORE] = jnp.where(keep, q, ext.astype(BF16))
        qv_ref[c, _VAR_AFTER] = jnp.where(keep, q, (-ext).astype(BF16))
        qv_ref[c, _VAR_PLAIN] = jnp.where(keep, q, jnp.zeros_like(q))
        qv_ref[c, _VAR_MASK] = _extras(lane, base, (-_MASK_SCORE * one, zero, zero),
                                       (zero,) * 3).astype(BF16)
    ccol = lax.broadcasted_iota(jnp.int32, (tq, tq), 1)
    crow = lax.broadcasted_iota(jnp.int32, (tq, tq), 0)
    own_bias = -slope * jnp.abs(crow - ccol).astype(F32)

    def update(carry_c, s_list, vx):
        m, accx = carry_c
        tiles = [s[:, t * LANES:(t + 1) * LANES] for s in s_list for t in range(s.shape[1] // LANES)]
        m_new = jnp.maximum(m, jnp.max(functools.reduce(jnp.maximum, tiles), axis=-1, keepdims=True))
        pv = None
        row = 0
        for s in s_list:
            d = jnp.dot(jnp.exp2(s - m_new).astype(BF16), vx[row:row + s.shape[1]],
                        preferred_element_type=F32)
            pv = d if pv is None else pv + d
            row += s.shape[1]
        return m_new, jnp.exp2(m - m_new) * accx + pv

    def scores(jb, dst_ref):
        for c in range(2):
            for kk in range(nch):
                g = jb * nch + kk
                var = jnp.where(g < i, _VAR_BEFORE, jnp.where(g == i, _VAR_MASK, _VAR_AFTER))
                kc = kp_ref[c, pl.ds(pl.multiple_of(g * tq, tq), tq), :]
                dst_ref[c, kk] = lax.dot_general(qv_ref[c, var], kc, _NT,
                                                 preferred_element_type=F32)

    def absorb(jb, src_ref, carry):
        vx = vx_ref[pl.ds(pl.multiple_of(jb * (nch * tq), nch * tq), nch * tq), :]
        return tuple(update(carry[c], [src_ref[c, kk] for kk in range(nch)], vx) for c in range(2))

    nsb = S // (nch * tq)
    dv = v_ref.shape[1]
    carry = tuple((jnp.full((tq, 1), NEG, F32), jnp.zeros((tq, 2 * dv), F32)) for _ in range(2))
    scores(0, sa_ref)

    def pair(t, carry):
        scores(2 * t + 1, sb_ref)
        carry = absorb(2 * t, sa_ref, carry)
        scores(2 * t + 2, sa_ref)
        return absorb(2 * t + 1, sb_ref, carry)

    carry = lax.fori_loop(0, nsb // 2 - 1, pair, carry)
    scores(nsb - 1, sb_ref)
    carry = absorb(nsb - 2, sa_ref, carry)

    meta_mask = jnp.where(lax.broadcasted_iota(jnp.int32, (tq, tkm), 1) < n_meta, 0.0, NEG)
    own_rows = pl.ds(pl.multiple_of(i * tq, tq), tq)
    tail_s = []
    for c in range(2):
        s_own = lax.dot_general(qv_ref[c, _VAR_PLAIN], kp_ref[c, own_rows, :], _NT,
                                preferred_element_type=F32) + own_bias
        s_meta = lax.dot_general(qv_ref[c, _VAR_BEFORE], kp_ref[c, pl.ds(S, tkm), :], _NT,
                                 preferred_element_type=F32) + meta_mask
        tail_s.append([s_own, s_meta])
    carry = absorb(nsb - 1, sb_ref, carry)
    vx_tail = jnp.concatenate([vx_ref[own_rows, :], vx_ref[pl.ds(S, tkm), :]], axis=0)
    carry = tuple(update(carry[c], tail_s[c], vx_tail) for c in range(2))

    lv = lamv_ref[...]
    lam = (jnp.exp(jnp.sum(lv[0:1] * lv[1:2], axis=-1, keepdims=True))
           - jnp.exp(jnp.sum(lv[2:3] * lv[3:4], axis=-1, keepdims=True)) + lam_init)
    (_, ax0), (_, ax1) = carry
    o = ax0[:, :dv] / ax0[:, dv:dv + 1] - lam * (ax1[:, :dv] / ax1[:, dv:dv + 1])
    o = o * lax.rsqrt(jnp.mean(o * o, axis=-1, keepdims=True) + EPS) * og_ref[...] * (1.0 - lam_init)
    o_ref[...] = o.astype(o_ref.dtype)


def _diff_attention(proj, projm, slopes, lamv, og, *, B, S, n_meta, H, hd, q_col, k_col, v_col,
                    lam_init, tq, nch):
    dv = 2 * hd
    nq = S // tq
    assert S % (2 * nch * tq) == 0 and dv == LANES
    body = functools.partial(_da_body, n_meta=n_meta, tq=tq, nch=nch, hd=hd, lam_init=lam_init)
    qb, kb, vb = q_col // dv, k_col // dv, v_col // dv
    tkm = projm.shape[0]
    return pl.pallas_call(
        body,
        grid=(B, H, nq),
        in_specs=[
            pl.BlockSpec(memory_space=pltpu.SMEM),
            pl.BlockSpec((tq, dv), lambda b, h, i: (b * nq + i, qb + h)),
            pl.BlockSpec((S, dv), lambda b, h, i: (b, kb + h)),
            pl.BlockSpec((S, dv), lambda b, h, i: (b, vb + h)),
            pl.BlockSpec((tkm, dv), lambda b, h, i: (0, kb + h)),
            pl.BlockSpec((tkm, dv), lambda b, h, i: (0, vb + h)),
            pl.BlockSpec(lamv.shape, lambda b, h, i: (0, 0)),
            pl.BlockSpec((1, dv), lambda b, h, i: (0, 0)),
        ],
        out_specs=pl.BlockSpec((tq, dv), lambda b, h, i: (b * nq + i, h)),
        out_shape=jax.ShapeDtypeStruct((B * S, H * dv), BF16),
        scratch_shapes=[pltpu.VMEM((2, S + tkm, dv), BF16),
                        pltpu.VMEM((S + tkm, 2 * dv), BF16),
                        pltpu.VMEM((2, 4, tq, dv), BF16),
                        pltpu.VMEM((2, nch, tq, tq), F32),
                        pltpu.VMEM((2, nch, tq, tq), F32)],
        compiler_params=_cparams(("parallel", "parallel", "arbitrary")),
        name="diff_attention",
    )(slopes, proj, proj, proj, projm, projm, lamv, og)


def _na_bias_table(rpb, rows, qrows):
    W = GRID_W
    H, n_dr, n_dc = rpb.shape
    n_blk = rows // qrows
    kh = min(NA_WIN_H, rows)
    qc = np.arange(W)[:, None]
    kc = np.arange(W)[None, :]
    cs = np.clip(qc - NA_WIN_W // 2, 0, W - NA_WIN_W)
    col_ok = (kc >= cs) & (kc < cs + NA_WIN_W)
    onehot = ((kc - qc + (NA_WIN_W - 1))[None] == np.arange(n_dc)[:, None, None]) & col_ok[None]
    tiles = jnp.einsum('hrd,dqk->hrqk', rpb.astype(F32), jnp.asarray(onehot, F32),
                       precision=lax.Precision.HIGHEST)
    tiles = jnp.where(jnp.asarray(col_ok), tiles, NEG)
    tiles = jnp.concatenate([tiles, jnp.full((H, 1, W, W), NEG, F32)], axis=1)
    sel = np.full((3, qrows, 3 * qrows), n_dr, np.int32)
    for v, (qb, kb0) in enumerate(((0, 0), (1, 0), (n_blk - 1, n_blk - 3))):
        r = (qb * qrows + np.arange(qrows))[:, None]
        kr = (kb0 * qrows + np.arange(3 * qrows))[None, :]
        r0 = np.clip(r - kh // 2, 0, rows - kh)
        row_ok = (kr >= r0) & (kr < r0 + kh)
        sel[v] = np.where(row_ok, kr - r + (NA_WIN_H - 1), n_dr)
    g = jnp.take(tiles, jnp.asarray(sel.reshape(-1)), axis=1)
    g = g.reshape(H, 3, qrows, 3 * qrows, W, W).transpose(1, 0, 2, 4, 3, 5)
    return g.reshape(3, H, qrows * W, 3 * qrows * W)


def _na_body(q_ref, k0_ref, k1_ref, k2_ref, v0_ref, v1_ref, v2_ref, km_ref, vm_ref, bias_ref,
             o_ref, *, hd, n_meta, nh):
    q = q_ref[...]
    k = jnp.concatenate([k0_ref[...], k1_ref[...], k2_ref[...]], axis=0)
    v = jnp.concatenate([v0_ref[...], v1_ref[...], v2_ref[...]], axis=0)
    km = km_ref[...]
    vm = vm_ref[...]
    lane = lax.broadcasted_iota(jnp.int32, q.shape, 1)
    zero = jnp.zeros_like(q)
    midx = lax.broadcasted_iota(jnp.int32, (1, km.shape[0]), 1)
    mmask = jnp.where(midx < n_meta, 0.0, NEG)
    out = jnp.zeros(q.shape, F32)
    for hh in range(nh):
        mine = (lane >= hh * hd) & (lane < (hh + 1) * hd)
        qh = jnp.where(mine, q, zero)
        s = lax.dot_general(qh, k, _NT, preferred_element_type=F32) + bias_ref[0, hh]
        sm = lax.dot_general(qh, km, _NT, preferred_element_type=F32) + mmask
        m = jnp.maximum(jnp.max(s, axis=-1, keepdims=True), jnp.max(sm, axis=-1, keepdims=True))
        p = jnp.exp(s - m)
        pm = jnp.exp(sm - m)
        l = jnp.sum(p, axis=-1, keepdims=True) + jnp.sum(pm, axis=-1, keepdims=True)
        o = (jnp.dot(p.astype(BF16), v, preferred_element_type=F32)
             + jnp.dot(pm.astype(BF16), vm, preferred_element_type=F32))
        out = jnp.where(mine, o / l, out)
    o_ref[...] = out.astype(o_ref.dtype)


def _nbhd_attention(proj, projm, bias_tab, *, B, S, n_meta, H, hd, q_col, k_col, v_col, qrows, nh):
    Q = qrows * GRID_W
    nqb = S // Q
    wb = nh * hd
    qc, kc, vc = q_col // wb, k_col // wb, v_col // wb
    mrows = projm.shape[0]

    def kb0(i):
        return jnp.clip(i - 1, 0, nqb - 3)

    def variant(i):
        return jnp.where(i == 0, 0, jnp.where(i == nqb - 1, 2, 1))

    def kv_spec(col, d):
        return pl.BlockSpec((Q, wb), lambda hg, b, i: (b * nqb + kb0(i) + d, col + hg))

    def meta_spec(col):
        return pl.BlockSpec((mrows, wb), lambda hg, b, i: (0, col + hg))

    body = functools.partial(_na_body, hd=hd, n_meta=n_meta, nh=nh)
    return pl.pallas_call(
        body,
        grid=(H // nh, B, nqb),
        in_specs=[pl.BlockSpec((Q, wb), lambda hg, b, i: (b * nqb + i, qc + hg)),
                  kv_spec(kc, 0), kv_spec(kc, 1), kv_spec(kc, 2),
                  kv_spec(vc, 0), kv_spec(vc, 1), kv_spec(vc, 2),
                  meta_spec(kc), meta_spec(vc),
                  pl.BlockSpec((1, nh, Q, 3 * Q), lambda hg, b, i: (variant(i), hg, 0, 0))],
        out_specs=pl.BlockSpec((Q, wb), lambda hg, b, i: (b * nqb + i, hg)),
        out_shape=jax.ShapeDtypeStruct((B * S, H * hd), BF16),
        compiler_params=_cparams(("parallel", "parallel", "arbitrary")),
        name="nbhd_attention",
    )(proj, proj, proj, proj, proj, proj, proj, projm, projm, bias_tab)


def _mix_body(oda_ref, ona_ref, gda_ref, gna_ref, h_ref, wbd_ref, wbn_ref, wo_ref, g2_ref,
              h1_ref, xn2_ref):
    yda = jnp.dot(oda_ref[...], wbd_ref[...], preferred_element_type=F32)
    yna = jnp.dot(ona_ref[...], wbn_ref[...], preferred_element_type=F32)
    mixed = gda_ref[...].astype(F32) * yda + gna_ref[...].astype(F32) * yna
    h1 = h_ref[...] + jnp.dot(mixed.astype(BF16), wo_ref[...], preferred_element_type=F32)
    h1_ref[...] = h1
    ms = jnp.mean(h1 * h1, axis=-1, keepdims=True)
    xn2_ref[...] = (h1 * lax.rsqrt(ms + EPS) * g2_ref[...]).astype(xn2_ref.dtype)


def _mix(o_da, o_na, proj, h0, wbd, wbn, wo, g2, *, gda_col, gna_col, tm):
    T, D = h0.shape
    const = lambda shape: pl.BlockSpec(shape, lambda i: (0, 0), pipeline_mode=pl.Buffered(1))
    return pl.pallas_call(
        _mix_body,
        grid=(T // tm,),
        in_specs=[
            pl.BlockSpec((tm, o_da.shape[1]), lambda i: (i, 0)),
            pl.BlockSpec((tm, o_na.shape[1]), lambda i: (i, 0)),
            pl.BlockSpec((tm, D), lambda i: (i, gda_col // D)),
            pl.BlockSpec((tm, D), lambda i: (i, gna_col // D)),
            pl.BlockSpec((tm, D), lambda i: (i, 0)),
            const(wbd.shape), const(wbn.shape), const(wo.shape), const(g2.shape),
        ],
        out_specs=[pl.BlockSpec((tm, D), lambda i: (i, 0)),
                   pl.BlockSpec((tm, D), lambda i: (i, 0))],
        out_shape=[jax.ShapeDtypeStruct((T, D), F32), jax.ShapeDtypeStruct((T, D), BF16)],
        compiler_params=_cparams(("parallel",)),
        name="mix_out",
    )(o_da, o_na, proj, proj, h0, wbd, wbn, wo, g2)


def _cand_pairs(k):
    return [(i, j) for i in range(k) for j in range(k) if (i + 1) * (j + 1) <= k]


def _topk_cols(s, k, exact_ties):
    n = s.shape[0]
    rows = lax.broadcasted_iota(jnp.int32, s.shape, 0).astype(F32)
    rank = jnp.full(s.shape, float(k), F32)
    vals = []
    for r in range(k):
        m = jnp.max(s, axis=0, keepdims=True)
        hit = s == m
        if exact_ties:
            first = jnp.min(jnp.where(hit, rows, float(n)), axis=0, keepdims=True)
            hit = rows == first
        rank = jnp.where(hit, float(r), rank)
        s = jnp.where(hit, -jnp.inf, s)
        vals.append(m)
    if exact_ties:
        return vals, rank, jnp.ones((1, s.shape[1]), F32)
    want = float(k * (k - 1) // 2 + (n - k) * k)
    ok = (jnp.sum(rank, axis=0, keepdims=True) == want).astype(F32)
    return vals, rank, ok


def _route_head(s1, s2, topk, pairs, exact_ties):
    npad = -(-len(pairs) // 8) * 8
    tm = s1.shape[1]
    v1, rank1, ok1 = _topk_cols(s1, topk, exact_ties)
    v2, rank2, ok2 = _topk_cols(s2, topk, exact_ties)
    cand = jnp.concatenate([v1[i] + v2[j] for i, j in pairs]
                           + [jnp.full((npad - len(pairs), tm), -jnp.inf, F32)], axis=0)
    ex = jnp.exp(cand - (v1[0] + v2[0]))
    _, crank, okc = _topk_cols(cand, topk, exact_ties)
    chosen = (crank < float(topk)).astype(F32)
    z = jnp.sum(chosen * ex, axis=0, keepdims=True)
    jn = jnp.zeros(s1.shape, F32)
    for i in range(topk):
        rows_i = [r for r, (pi, _) in enumerate(pairs) if pi == i]
        cnt = functools.reduce(lambda a, b: a + b, [chosen[r:r + 1] for r in rows_i])
        jn = jnp.where(rank1 == float(i), cnt, jn)
    tables = (rank2, jnp.exp(s2 - v2[0]), jn, jnp.exp(s1 - v1[0]) / z)
    return tables, jnp.min(ok1 * ok2 * okc) > 0.5


def _route_body(xn_ref, wq_ref, sk_ref, r2_ref, e2_ref, jn_ref, e1_ref, q_sc, *, heads, nk, topk):
    q = jnp.dot(xn_ref[...], wq_ref[...], preferred_element_type=F32).astype(BF16)
    for h in range(heads):
        q_sc[h] = q[:, h * 2 * nk:(h + 1) * 2 * nk]
    pairs = _cand_pairs(topk)

    def per_head(h, _):
        qh = q_sc[h]
        s1 = lax.dot_general(sk_ref[0], qh[:, :nk], _NT, preferred_element_type=F32)
        s2 = lax.dot_general(sk_ref[1], qh[:, nk:], _NT, preferred_element_type=F32)

        def write(tables):
            for ref, val in zip((r2_ref, e2_ref, jn_ref, e1_ref), tables):
                ref[h] = val.astype(ref.dtype)

        tables, ok = _route_head(s1, s2, topk, pairs, exact_ties=False)
        write(tables)

        @pl.when(jnp.logical_not(ok))
        def _():
            write(_route_head(s1, s2, topk, pairs, exact_ties=True)[0])

        return 0

    lax.fori_loop(0, heads, per_head, 0)


def _peer_route(xn2, wq, sk, *, heads, nk, topk, tm):
    T, D = xn2.shape
    body = functools.partial(_route_body, heads=heads, nk=nk, topk=topk)
    out_f32 = jax.ShapeDtypeStruct((heads, nk, T), F32)
    out_bf16 = jax.ShapeDtypeStruct((heads, nk, T), BF16)
    ospec = pl.BlockSpec((heads, nk, tm), lambda i: (0, 0, i))
    return pl.pallas_call(
        body,
        grid=(T // tm,),
        in_specs=[pl.BlockSpec((tm, D), lambda i: (i, 0)),
                  pl.BlockSpec(wq.shape, lambda i: (0, 0), pipeline_mode=pl.Buffered(1)),
                  pl.BlockSpec(sk.shape, lambda i: (0, 0, 0))],
        out_specs=[ospec] * 4,
        out_shape=[out_bf16, out_bf16, out_f32, out_f32],
        scratch_shapes=[pltpu.VMEM((heads, tm, 2 * nk), BF16)],
        compiler_params=_cparams(("parallel",)),
        name="peer_route",
    )(xn2, wq, sk)


def _gelu_tanh(x):
    return 0.5 * x * (1.0 + jnp.tanh(math.sqrt(2.0 / math.pi) * (x + 0.044715 * (x * x * x))))


def _peer_body(x_ref, u_ref, vt_ref, r2_ref, e2_ref, jn_ref, e1_ref, h1_ref, o_ref, acc_ref, *,
               heads, nk, nb, cw):
    j = pl.program_id(1)
    x = x_ref[...]
    nchunk = nb * nk // cw

    def scores(c):
        return lax.dot_general(u_ref[c * cw:(c + 1) * cw, :], x, _NT, preferred_element_type=F32)

    a_next = scores(0)
    hts = []
    for c in range(nchunk):
        a = a_next
        if c + 1 < nchunk:
            a_next = scores(c + 1)
        act = _gelu_tanh(a).astype(BF16)
        ws = []
        for nl in range(cw // nk):
            n1 = c * (cw // nk) + nl
            w = None
            for h in range(heads):
                jn = jn_ref[h, n1:n1 + 1, :].astype(BF16)
                e1 = e1_ref[h, n1:n1 + 1, :].astype(BF16)
                t = jnp.where(r2_ref[h] < jn, e2_ref[h] * e1, jnp.zeros((), BF16))
                w = t if w is None else w + t
            ws.append(w)
        hts.append(jnp.concatenate(ws, axis=0) * act)
    total = jnp.dot(vt_ref[...], jnp.concatenate(hts, axis=0), preferred_element_type=F32)

    @pl.when(j == 0)
    def _():
        acc_ref[...] = total

    @pl.when(j > 0)
    def _():
        acc_ref[...] += total

    @pl.when(j == pl.num_programs(1) - 1)
    def _():
        o_ref[...] = h1_ref[...] + acc_ref[...].T


def _peer_experts(xn2, u_bf, vt_bf, r2, e2, jn, e1, h1, *, heads, nk, nb, tm):
    T, D = xn2.shape
    E = u_bf.shape[0]
    eb = nb * nk
    body = functools.partial(_peer_body, heads=heads, nk=nk, nb=nb, cw=MXU_DIM)
    key_spec = pl.BlockSpec((heads, nk, tm), lambda i, j: (0, 0, i))
    row_spec = pl.BlockSpec((heads, nb, tm), lambda i, j: (0, j, i))
    return pl.pallas_call(
        body,
        grid=(T // tm, E // eb),
        in_specs=[pl.BlockSpec((tm, D), lambda i, j: (i, 0)),
                  pl.BlockSpec((eb, D), lambda i, j: (j, 0)),
                  pl.BlockSpec((D, eb), lambda i, j: (0, j)),
                  key_spec, key_spec, row_spec, row_spec,
                  pl.BlockSpec((tm, D), lambda i, j: (i, 0))],
        out_specs=pl.BlockSpec((tm, D), lambda i, j: (i, 0)),
        out_shape=jax.ShapeDtypeStruct((T, D), F32),
        scratch_shapes=[pltpu.VMEM((D, tm), F32)],
        compiler_params=_cparams(("parallel", "arbitrary")),
        name="peer_experts",
    )(xn2, u_bf, vt_bf, r2, e2, jn, e1, h1)


def kernel(x, meta_tokens, norm1_g, w_in, da_q_norm_g, da_k_norm_g, lambda_q1, lambda_k1,
           lambda_q2, lambda_k2, da_out_norm_g, na_q_norm_g, na_k_norm_g, na_rpb,
           w_branch_da, w_branch_na, w_out, norm2_g, peer_w_query, peer_sub_keys, peer_u, peer_v):
    B, S, D = x.shape
    n_meta = meta_tokens.shape[0]
    da_hd = da_q_norm_g.shape[1]
    da_vw = w_branch_da.shape[1]
    da_heads = da_vw // (2 * da_hd)
    da_qkw = da_heads * 2 * da_hd
    na_hd = na_q_norm_g.shape[1]
    na_w = w_branch_na.shape[1]
    na_heads = na_rpb.shape[1]
    nk = peer_sub_keys.shape[2]
    peer_heads = peer_w_query.shape[2] // (2 * nk)
    splits = (da_qkw, da_qkw, da_vw, na_w, na_w, na_w, D, D)
    col = np.concatenate([[0], np.cumsum(splits)]).tolist()
    assert w_in.shape[0] == 1
    assert w_in.shape[2] == col[-1] and S % ROW_BLOCK == 0 and S % GRID_W == 0
    assert da_hd == na_hd and 2 * da_hd == LANES and nk == LANES and n_meta <= META_ROWS
    l = 0

    T = B * S
    tn = 512
    tm_proj = max(t for t in (1024, 512, 256) if T % t == 0)
    tm_peer = 512 if T % 512 == 0 else 256

    xr = x.reshape(T, D)
    xm = jnp.pad(meta_tokens.astype(x.dtype), ((0, META_ROWS - n_meta), (0, 0)))
    log2e = math.log2(math.e)
    slopes = jnp.asarray(2.0 ** (-8.0 * np.arange(1, da_heads + 1) / da_heads) * log2e, dtype=F32)
    ones = lambda n: jnp.ones((n,), F32)

    gcols = jnp.concatenate([
        jnp.tile(da_q_norm_g[l].astype(F32), da_qkw // da_hd) * (da_hd ** -0.5 * log2e),
        jnp.tile(da_k_norm_g[l].astype(F32), da_qkw // da_hd),
        ones(da_vw),
        jnp.tile(na_q_norm_g[l].astype(F32), na_heads) * na_hd ** -0.5,
        jnp.tile(na_k_norm_g[l].astype(F32), na_heads),
        ones(na_w + 2 * D)])[None]
    norm_ranges = ((col[0] // tn, col[2] // tn), (col[3] // tn, col[5] // tn))
    proj_args = dict(norm_ranges=norm_ranges, sig_start=col[6] // tn, head_dim=da_hd, tn=tn)
    g1 = norm1_g[l][None].astype(F32)
    w_in_bf = w_in[l].astype(BF16)
    proj = _inproj(xr, g1, w_in_bf, gcols, tm=tm_proj, **proj_args)
    projm = _inproj(xm, g1, w_in_bf, gcols, tm=META_ROWS, **proj_args)

    lam_init = 0.8 - 0.6 * math.exp(-0.3 * l)
    lamv = jnp.stack([lambda_q1[l], lambda_k1[l], lambda_q2[l], lambda_k2[l]]).astype(F32)
    o_da = _diff_attention(proj, projm, slopes, lamv, da_out_norm_g[l][None].astype(F32),
                           B=B, S=S, n_meta=n_meta, H=da_heads, hd=da_hd,
                           q_col=col[0], k_col=col[1], v_col=col[2], lam_init=lam_init,
                           tq=ROW_BLOCK, nch=8)

    qrows = ROW_BLOCK // GRID_W
    bias_tab = _na_bias_table(na_rpb[l], S // GRID_W, qrows)
    o_na = _nbhd_attention(proj, projm, bias_tab, B=B, S=S, n_meta=n_meta, H=na_heads, hd=na_hd,
                           q_col=col[3], k_col=col[4], v_col=col[5], qrows=qrows, nh=4)

    h1, xn2 = _mix(o_da, o_na, proj, xr, w_branch_da[l].astype(BF16),
                   w_branch_na[l].astype(BF16), w_out[l].astype(BF16),
                   norm2_g[l][None].astype(F32), gda_col=col[6], gna_col=col[7], tm=ROW_BLOCK)

    r2, e2, jn, e1 = _peer_route(xn2, peer_w_query[l].astype(BF16),
                                 peer_sub_keys[l].astype(BF16), heads=peer_heads, nk=nk,
                                 topk=PEER_TOPK, tm=ROW_BLOCK)
    out = _peer_experts(xn2, peer_u[l].astype(BF16), peer_v[l].T.astype(BF16), r2, e2, jn, e1, h1,
                        heads=peer_heads, nk=nk, nb=8, tm=tm_peer)
    return out.reshape(B, S, D)
```

```python
import functools
import math

import numpy as np
import jax
import jax.numpy as jnp
from jax import lax
from jax.experimental import pallas as pl
from jax.experimental.pallas import tpu as pltpu

F32 = jnp.float32
BF16 = jnp.bfloat16

GRID_W = 64
NA_WIN_H = 8
NA_WIN_W = 16
PEER_TOPK = 16
EPS = 1e-6
NEG = -1e30

LANES = 128
MXU_DIM = 256
VMEM_LIMIT = 56 * 1024 * 1024

ROW_BLOCK = 256
META_ROWS = LANES

_NT = (((1,), (1,)), ((), ()))


def _cparams(sem):
    return pltpu.CompilerParams(dimension_semantics=sem, vmem_limit_bytes=VMEM_LIMIT)


def _inproj_body(x_ref, g1_ref, w_ref, gc_ref, o_ref, xn_ref, *, norm_ranges, sig_start, head_dim):
    j = pl.program_id(1)

    @pl.when(j == 0)
    def _():
        x = x_ref[...]
        ms = jnp.mean(x * x, axis=-1, keepdims=True)
        xn_ref[...] = (x * lax.rsqrt(ms + EPS) * g1_ref[...]).astype(BF16)

    tn = o_ref.shape[1]
    chunks = [slice(t * MXU_DIM, (t + 1) * MXU_DIM) for t in range(tn // MXU_DIM)]

    def cols(sl):
        return jnp.dot(xn_ref[...], w_ref[:, sl], preferred_element_type=F32)

    is_norm = functools.reduce(jnp.logical_or, [(j >= a) & (j < b) for a, b in norm_ranges])
    is_sig = j >= sig_start

    @pl.when(is_norm)
    def _():
        r = lax.broadcasted_iota(jnp.int32, (MXU_DIM, MXU_DIM), 0) // head_dim
        c = lax.broadcasted_iota(jnp.int32, (MXU_DIM, MXU_DIM), 1) // head_dim
        ones_bd = (r == c).astype(BF16)
        y = jnp.dot(xn_ref[...], w_ref[...], preferred_element_type=F32)
        yy = (y * y).astype(BF16)
        ss = jnp.concatenate([jnp.dot(yy[:, sl], ones_bd, preferred_element_type=F32)
                              for sl in chunks], axis=1)
        o_ref[...] = (y * lax.rsqrt(ss * (1.0 / head_dim) + EPS) * gc_ref[...]).astype(o_ref.dtype)

    @pl.when(is_sig)
    def _():
        for sl in chunks:
            o_ref[:, sl] = jax.nn.sigmoid(cols(sl)).astype(o_ref.dtype)

    @pl.when(jnp.logical_not(is_norm | is_sig))
    def _():
        for sl in chunks:
            o_ref[:, sl] = cols(sl).astype(o_ref.dtype)


def _inproj(x2d, g1, w_bf, gcols, *, norm_ranges, sig_start, head_dim, tm, tn):
    T, D = x2d.shape
    C = w_bf.shape[1]
    body = functools.partial(_inproj_body, norm_ranges=norm_ranges, sig_start=sig_start,
                             head_dim=head_dim)
    return pl.pallas_call(
        body,
        grid=(T // tm, C // tn),
        in_specs=[
            pl.BlockSpec((tm, D), lambda i, j: (i, 0)),
            pl.BlockSpec((1, D), lambda i, j: (0, 0)),
            pl.BlockSpec((D, tn), lambda i, j: (0, j)),
            pl.BlockSpec((1, tn), lambda i, j: (0, j)),
        ],
        out_specs=pl.BlockSpec((tm, tn), lambda i, j: (i, j)),
        out_shape=jax.ShapeDtypeStruct((T, C), BF16),
        scratch_shapes=[pltpu.VMEM((tm, D), BF16)],
        compiler_params=_cparams(("parallel", "arbitrary")),
        name="inproj",
    )(x2d, g1, w_bf, gcols)


_VAR_BEFORE, _VAR_AFTER, _VAR_PLAIN, _VAR_MASK = 0, 1, 2, 3
_MASK_SCORE = 32768.0


def _split3(x):
    hi = x.astype(BF16).astype(F32)
    mid = (x - hi).astype(BF16).astype(F32)
    lo = (x - hi - mid).astype(BF16).astype(F32)
    return hi, mid, lo


def _extras(lane, base, first3, last3):
    out = jnp.zeros(lane.shape, F32)
    for t, val in enumerate(tuple(first3) + tuple(last3)):
        out = jnp.where(lane == base + t, val, out)
    return out


def _da_body(slopes_ref, q_ref, k_ref, v_ref, km_ref, vm_ref, lamv_ref, og_ref, o_ref,
             kp_ref, vx_ref, qv_ref, sa_ref, sb_ref, *, n_meta, tq, nch, hd, lam_init):
    h = pl.program_id(1)
    i = pl.program_id(2)
    slope = slopes_ref[h]
    S = k_ref.shape[0]
    tkm = km_ref.shape[0]
    one = jnp.ones((), F32)
    zero = jnp.zeros((), F32)

    @pl.when(i == 0)
    def _():
        for rows, ksrc, vsrc, pos0 in ((pl.ds(0, S), k_ref, v_ref, n_meta),
                                       (pl.ds(S, tkm), km_ref, vm_ref, 0)):
            k = ksrc[...]
            lane = lax.broadcasted_iota(jnp.int32, k.shape, 1)
            kpos = (lax.broadcasted_iota(jnp.int32, k.shape, 0) + pos0).astype(F32)
            b3 = _split3(slope * kpos)
            kp_ref[0, rows, :] = jnp.where(lane < hd, k, _extras(lane, hd, (one,) * 3, b3).astype(BF16))
            kp_ref[1, rows, :] = jnp.where(lane >= hd, k, _extras(lane, 0, (one,) * 3, b3).astype(BF16))
            vx_ref[rows, :LANES] = vsrc[...]
            vx_ref[rows, LANES:] = jnp.where(lane == 0, 1.0, 0.0).astype(BF16)

    q = q_ref[...]
    lane = lax.broadcasted_iota(jnp.int32, (tq, LANES), 1)
    qpos = (i * tq + n_meta + lax.broadcasted_iota(jnp.int32, (tq, LANES), 0)).astype(F32)
    a3 = _split3(-slope * qpos)
    for c, base in ((0, hd), (1, 0)):
        keep = (lane < hd) if c == 0 else (lane >= hd)
        ext = _extras(lane, base, a3, (one,) * 3)
        qv_ref[c, _VAR_BEFORE] = jnp.where(keep, q, ext.astype(BF16))
        qv_ref[c, _VAR_AFTER] = jnp.where(keep, q, (-ext).astype(BF16))
        qv_ref[c, _VAR_PLAIN] = jnp.where(keep, q, jnp.zeros_like(q))
        qv_ref[c, _VAR_MASK] = _extras(lane, base, (-_MASK_SCORE * one, zero, zero),
                                       (zero,) * 3).astype(BF16)
    ccol = lax.broadcasted_iota(jnp.int32, (tq, tq), 1)
    crow = lax.broadcasted_iota(jnp.int32, (tq, tq), 0)
    own_bias = -slope * jnp.abs(crow - ccol).astype(F32)

    def update(carry_c, s_list, vx):
        m, accx = carry_c
        tiles = [s[:, t * LANES:(t + 1) * LANES] for s in s_list for t in range(s.shape[1] // LANES)]
        m_new = jnp.maximum(m, jnp.max(functools.reduce(jnp.maximum, tiles), axis=-1, keepdims=True))
        pv = None
        row = 0
        for s in s_list:
            d = jnp.dot(jnp.exp2(s - m_new).astype(BF16), vx[row:row + s.shape[1]],
                        preferred_element_type=F32)
            pv = d if pv is None else pv + d
            row += s.shape[1]
        return m_new, jnp.exp2(m - m_new) * accx + pv

    def scores(jb, dst_ref):
        for c in range(2):
            for kk in range(nch):
                g = jb * nch + kk
                var = jnp.where(g < i, _VAR_BEFORE, jnp.where(g == i, _VAR_MASK, _VAR_AFTER))
                kc = kp_ref[c, pl.ds(pl.multiple_of(g * tq, tq), tq), :]
                dst_ref[c, kk] = lax.dot_general(qv_ref[c, var], kc, _NT,
                                                 preferred_element_type=F32)

    def absorb(jb, src_ref, carry):
        vx = vx_ref[pl.ds(pl.multiple_of(jb * (nch * tq), nch * tq), nch * tq), :]
        return tuple(update(carry[c], [src_ref[c, kk] for kk in range(nch)], vx) for c in range(2))

    nsb = S // (nch * tq)
    dv = v_ref.shape[1]
    carry = tuple((jnp.full((tq, 1), NEG, F32), jnp.zeros((tq, 2 * dv), F32)) for _ in range(2))
    scores(0, sa_ref)

    def pair(t, carry):
        scores(2 * t + 1, sb_ref)
        carry = absorb(2 * t, sa_ref, carry)
        scores(2 * t + 2, sa_ref)
        return absorb(2 * t + 1, sb_ref, carry)

    carry = lax.fori_loop(0, nsb // 2 - 1, pair, carry)
    scores(nsb - 1, sb_ref)
    carry = absorb(nsb - 2, sa_ref, carry)

    meta_mask = jnp.where(lax.broadcasted_iota(jnp.int32, (tq, tkm), 1) < n_meta, 0.0, NEG)
    own_rows = pl.ds(pl.multiple_of(i * tq, tq), tq)
    tail_s = []
    for c in range(2):
        s_own = lax.dot_general(qv_ref[c, _VAR_PLAIN], kp_ref[c, own_rows, :], _NT,
                                preferred_element_type=F32) + own_bias
        s_meta = lax.dot_general(qv_ref[c, _VAR_BEFORE], kp_ref[c, pl.ds(S, tkm), :], _NT,
                                 preferred_element_type=F32) + meta_mask
        tail_s.append([s_own, s_meta])
    carry = absorb(nsb - 1, sb_ref, carry)
    vx_tail = jnp.concatenate([vx_ref[own_rows, :], vx_ref[pl.ds(S, tkm), :]], axis=0)
    carry = tuple(update(carry[c], tail_s[c], vx_tail) for c in range(2))

    lv = lamv_ref[...]
    lam = (jnp.exp(jnp.sum(lv[0:1] * lv[1:2], axis=-1, keepdims=True))
           - jnp.exp(jnp.sum(lv[2:3] * lv[3:4], axis=-1, keepdims=True)) + lam_init)
    (_, ax0), (_, ax1) = carry
    o = ax0[:, :dv] / ax0[:, dv:dv + 1] - lam * (ax1[:, :dv] / ax1[:, dv:dv + 1])
    o = o * lax.rsqrt(jnp.mean(o * o, axis=-1, keepdims=True) + EPS) * og_ref[...] * (1.0 - lam_init)
    o_ref[...] = o.astype(o_ref.dtype)


def _diff_attention(proj, projm, slopes, lamv, og, *, B, S, n_meta, H, hd, q_col, k_col, v_col,
                    lam_init, tq, nch):
    dv = 2 * hd
    nq = S // tq
    assert S % (2 * nch * tq) == 0 and dv == LANES
    body = functools.partial(_da_body, n_meta=n_meta, tq=tq, nch=nch, hd=hd, lam_init=lam_init)
    qb, kb, vb = q_col // dv, k_col // dv, v_col // dv
    tkm = projm.shape[0]
    return pl.pallas_call(
        body,
        grid=(B, H, nq),
        in_specs=[
            pl.BlockSpec(memory_space=pltpu.SMEM),
            pl.BlockSpec((tq, dv), lambda b, h, i: (b * nq + i, qb + h)),
            pl.BlockSpec((S, dv), lambda b, h, i: (b, kb + h)),
            pl.BlockSpec((S, dv), lambda b, h, i: (b, vb + h)),
            pl.BlockSpec((tkm, dv), lambda b, h, i: (0, kb + h)),
            pl.BlockSpec((tkm, dv), lambda b, h, i: (0, vb + h)),
            pl.BlockSpec(lamv.shape, lambda b, h, i: (0, 0)),
            pl.BlockSpec((1, dv), lambda b, h, i: (0, 0)),
        ],
        out_specs=pl.BlockSpec((tq, dv), lambda b, h, i: (b * nq + i, h)),
        out_shape=jax.ShapeDtypeStruct((B * S, H * dv), BF16),
        scratch_shapes=[pltpu.VMEM((2, S + tkm, dv), BF16),
                        pltpu.VMEM((S + tkm, 2 * dv), BF16),
                        pltpu.VMEM((2, 4, tq, dv), BF16),
                        pltpu.VMEM((2, nch, tq, tq), F32),
                        pltpu.VMEM((2, nch, tq, tq), F32)],
        compiler_params=_cparams(("parallel", "parallel", "arbitrary")),
        name="diff_attention",
    )(slopes, proj, proj, proj, projm, projm, lamv, og)


def _na_bias_table(rpb, rows, qrows):
    W = GRID_W
    H, n_dr, n_dc = rpb.shape
    n_blk = rows // qrows
    kh = min(NA_WIN_H, rows)
    qc = np.arange(W)[:, None]
    kc = np.arange(W)[None, :]
    cs = np.clip(qc - NA_WIN_W // 2, 0, W - NA_WIN_W)
    col_ok = (kc >= cs) & (kc < cs + NA_WIN_W)
    onehot = ((kc - qc + (NA_WIN_W - 1))[None] == np.arange(n_dc)[:, None, None]) & col_ok[None]
    tiles = jnp.einsum('hrd,dqk->hrqk', rpb.astype(F32), jnp.asarray(onehot, F32),
                       precision=lax.Precision.HIGHEST)
    tiles = jnp.where(jnp.asarray(col_ok), tiles, NEG)
    tiles = jnp.concatenate([tiles, jnp.full((H, 1, W, W), NEG, F32)], axis=1)
    sel = np.full((3, qrows, 3 * qrows), n_dr, np.int32)
    for v, (qb, kb0) in enumerate(((0, 0), (1, 0), (n_blk - 1, n_blk - 3))):
        r = (qb * qrows + np.arange(qrows))[:, None]
        kr = (kb0 * qrows + np.arange(3 * qrows))[None, :]
        r0 = np.clip(r - kh // 2, 0, rows - kh)
        row_ok = (kr >= r0) & (kr < r0 + kh)
        sel[v] = np.where(row_ok, kr - r + (NA_WIN_H - 1), n_dr)
    g = jnp.take(tiles, jnp.asarray(sel.reshape(-1)), axis=1)
    g = g.reshape(H, 3, qrows, 3 * qrows, W, W).transpose(1, 0, 2, 4, 3, 5)
    return g.reshape(3, H, qrows * W, 3 * qrows * W)


def _na_body(q_ref, k0_ref, k1_ref, k2_ref, v0_ref, v1_ref, v2_ref, km_ref, vm_ref, bias_ref,
             o_ref, *, hd, n_meta, nh):
    q = q_ref[...]
    k = jnp.concatenate([k0_ref[...], k1_ref[...], k2_ref[...]], axis=0)
    v = jnp.concatenate([v0_ref[...], v1_ref[...], v2_ref[...]], axis=0)
    km = km_ref[...]
    vm = vm_ref[...]
    lane = lax.broadcasted_iota(jnp.int32, q.shape, 1)
    zero = jnp.zeros_like(q)
    midx = lax.broadcasted_iota(jnp.int32, (1, km.shape[0]), 1)
    mmask = jnp.where(midx < n_meta, 0.0, NEG)
    out = jnp.zeros(q.shape, F32)
    for hh in range(nh):
        mine = (lane >= hh * hd) & (lane < (hh + 1) * hd)
        qh = jnp.where(mine, q, zero)
        s = lax.dot_general(qh, k, _NT, preferred_element_type=F32) + bias_ref[0, hh]
        sm = lax.dot_general(qh, km, _NT, preferred_element_type=F32) + mmask
        m = jnp.maximum(jnp.max(s, axis=-1, keepdims=True), jnp.max(sm, axis=-1, keepdims=True))
        p = jnp.exp(s - m)
        pm = jnp.exp(sm - m)
        l = jnp.sum(p, axis=-1, keepdims=True) + jnp.sum(pm, axis=-1, keepdims=True)
        o = (jnp.dot(p.astype(BF16), v, preferred_element_type=F32)
             + jnp.dot(pm.astype(BF16), vm, preferred_element_type=F32))
        out = jnp.where(mine, o / l, out)
    o_ref[...] = out.astype(o_ref.dtype)


def _nbhd_attention(proj, projm, bias_tab, *, B, S, n_meta, H, hd, q_col, k_col, v_col, qrows, nh):
    Q = qrows * GRID_W
    nqb = S // Q
    wb = nh * hd
    qc, kc, vc = q_col // wb, k_col // wb, v_col // wb
    mrows = projm.shape[0]

    def kb0(i):
        return jnp.clip(i - 1, 0, nqb - 3)

    def variant(i):
        return jnp.where(i == 0, 0, jnp.where(i == nqb - 1, 2, 1))

    def kv_spec(col, d):
        return pl.BlockSpec((Q, wb), lambda hg, b, i: (b * nqb + kb0(i) + d, col + hg))

    def meta_spec(col):
        return pl.BlockSpec((mrows, wb), lambda hg, b, i: (0, col + hg))

    body = functools.partial(_na_body, hd=hd, n_meta=n_meta, nh=nh)
    return pl.pallas_call(
        body,
        grid=(H // nh, B, nqb),
        in_specs=[pl.BlockSpec((Q, wb), lambda hg, b, i: (b * nqb + i, qc + hg)),
                  kv_spec(kc, 0), kv_spec(kc, 1), kv_spec(kc, 2),
                  kv_spec(vc, 0), kv_spec(vc, 1), kv_spec(vc, 2),
                  meta_spec(kc), meta_spec(vc),
                  pl.BlockSpec((1, nh, Q, 3 * Q), lambda hg, b, i: (variant(i), hg, 0, 0))],
        out_specs=pl.BlockSpec((Q, wb), lambda hg, b, i: (b * nqb + i, hg)),
        out_shape=jax.ShapeDtypeStruct((B * S, H * hd), BF16),
        compiler_params=_cparams(("parallel", "parallel", "arbitrary")),
        name="nbhd_attention",
    )(proj, proj, proj, proj, proj, proj, proj, projm, projm, bias_tab)


def _mix_body(oda_ref, ona_ref, gda_ref, gna_ref, h_ref, wbd_ref, wbn_ref, wo_ref, g2_ref,
              h1_ref, xn2_ref):
    yda = jnp.dot(oda_ref[...], wbd_ref[...], preferred_element_type=F32)
    yna = jnp.dot(ona_ref[...], wbn_ref[...], preferred_element_type=F32)
    mixed = gda_ref[...].astype(F32) * yda + gna_ref[...].astype(F32) * yna
    h1 = h_ref[...] + jnp.dot(mixed.astype(BF16), wo_ref[...], preferred_element_type=F32)
    h1_ref[...] = h1
    ms = jnp.mean(h1 * h1, axis=-1, keepdims=True)
    xn2_ref[...] = (h1 * lax.rsqrt(ms + EPS) * g2_ref[...]).astype(xn2_ref.dtype)


def _mix(o_da, o_na, proj, h0, wbd, wbn, wo, g2, *, gda_col, gna_col, tm):
    T, D = h0.shape
    const = lambda shape: pl.BlockSpec(shape, lambda i: (0, 0), pipeline_mode=pl.Buffered(1))
    return pl.pallas_call(
        _mix_body,
        grid=(T // tm,),
        in_specs=[
            pl.BlockSpec((tm, o_da.shape[1]), lambda i: (i, 0)),
            pl.BlockSpec((tm, o_na.shape[1]), lambda i: (i, 0)),
            pl.BlockSpec((tm, D), lambda i: (i, gda_col // D)),
            pl.BlockSpec((tm, D), lambda i: (i, gna_col // D)),
            pl.BlockSpec((tm, D), lambda i: (i, 0)),
            const(wbd.shape), const(wbn.shape), const(wo.shape), const(g2.shape),
        ],
        out_specs=[pl.BlockSpec((tm, D), lambda i: (i, 0)),
                   pl.BlockSpec((tm, D), lambda i: (i, 0))],
        out_shape=[jax.ShapeDtypeStruct((T, D), F32), jax.ShapeDtypeStruct((T, D), BF16)],
        compiler_params=_cparams(("parallel",)),
        name="mix_out",
    )(o_da, o_na, proj, proj, h0, wbd, wbn, wo, g2)


def _cand_pairs(k):
    return [(i, j) for i in range(k) for j in range(k) if (i + 1) * (j + 1) <= k]


def _topk_cols(s, k, exact_ties):
    n = s.shape[0]
    rows = lax.broadcasted_iota(jnp.int32, s.shape, 0).astype(F32)
    rank = jnp.full(s.shape, float(k), F32)
    vals = []
    for r in range(k):
        m = jnp.max(s, axis=0, keepdims=True)
        hit = s == m
        if exact_ties:
            first = jnp.min(jnp.where(hit, rows, float(n)), axis=0, keepdims=True)
            hit = rows == first
        rank = jnp.where(hit, float(r), rank)
        s = jnp.where(hit, -jnp.inf, s)
        vals.append(m)
    if exact_ties:
        return vals, rank, jnp.ones((1, s.shape[1]), F32)
    want = float(k * (k - 1) // 2 + (n - k) * k)
    ok = (jnp.sum(rank, axis=0, keepdims=True) == want).astype(F32)
    return vals, rank, ok


def _route_head(s1, s2, topk, pairs, exact_ties):
    npad = -(-len(pairs) // 8) * 8
    tm = s1.shape[1]
    v1, rank1, ok1 = _topk_cols(s1, topk, exact_ties)
    v2, rank2, ok2 = _topk_cols(s2, topk, exact_ties)
    cand = jnp.concatenate([v1[i] + v2[j] for i, j in pairs]
                           + [jnp.full((npad - len(pairs), tm), -jnp.inf, F32)], axis=0)
    ex = jnp.exp(cand - (v1[0] + v2[0]))
    _, crank, okc = _topk_cols(cand, topk, exact_ties)
    chosen = (crank < float(topk)).astype(F32)
    z = jnp.sum(chosen * ex, axis=0, keepdims=True)
    jn = jnp.zeros(s1.shape, F32)
    for i in range(topk):
        rows_i = [r for r, (pi, _) in enumerate(pairs) if pi == i]
        cnt = functools.reduce(lambda a, b: a + b, [chosen[r:r + 1] for r in rows_i])
        jn = jnp.where(rank1 == float(i), cnt, jn)
    tables = (rank2, jnp.exp(s2 - v2[0]), jn, jnp.exp(s1 - v1[0]) / z)
    return tables, jnp.min(ok1 * ok2 * okc) > 0.5


def _route_body(xn_ref, wq_ref, sk_ref, r2_ref, e2_ref, jn_ref, e1_ref, q_sc, *, heads, nk, topk,
                heads_per_iter):
    q = jnp.dot(xn_ref[...], wq_ref[...], preferred_element_type=F32).astype(BF16)
    for h in range(heads):
        q_sc[h] = q[:, h * 2 * nk:(h + 1) * 2 * nk]
    pairs = _cand_pairs(topk)

    def one_head(h):
        qh = q_sc[h]
        s1 = lax.dot_general(sk_ref[0], qh[:, :nk], _NT, preferred_element_type=F32)
        s2 = lax.dot_general(sk_ref[1], qh[:, nk:], _NT, preferred_element_type=F32)

        def write(tables):
            for ref, val in zip((r2_ref, e2_ref, jn_ref, e1_ref), tables):
                ref[h] = val.astype(ref.dtype)

        tables, ok = _route_head(s1, s2, topk, pairs, exact_ties=False)
        write(tables)
        return ok, lambda: write(_route_head(s1, s2, topk, pairs, exact_ties=True)[0])

    def per_group(g, _):
        redo = [one_head(g * heads_per_iter + t) for t in range(heads_per_iter)]
        for ok, exact in redo:
            pl.when(jnp.logical_not(ok))(exact)
        return 0

    lax.fori_loop(0, heads // heads_per_iter, per_group, 0)


def _peer_route(xn2, wq, sk, *, heads, nk, topk, tm, heads_per_iter=8):
    T, D = xn2.shape
    body = functools.partial(_route_body, heads=heads, nk=nk, topk=topk,
                             heads_per_iter=heads_per_iter)
    out_f32 = jax.ShapeDtypeStruct((heads, nk, T), F32)
    out_bf16 = jax.ShapeDtypeStruct((heads, nk, T), BF16)
    ospec = pl.BlockSpec((heads, nk, tm), lambda i: (0, 0, i))
    return pl.pallas_call(
        body,
        grid=(T // tm,),
        in_specs=[pl.BlockSpec((tm, D), lambda i: (i, 0)),
                  pl.BlockSpec(wq.shape, lambda i: (0, 0), pipeline_mode=pl.Buffered(1)),
                  pl.BlockSpec(sk.shape, lambda i: (0, 0, 0))],
        out_specs=[ospec] * 4,
        out_shape=[out_bf16, out_bf16, out_f32, out_f32],
        scratch_shapes=[pltpu.VMEM((heads, tm, 2 * nk), BF16)],
        compiler_params=_cparams(("parallel",)),
        name="peer_route",
    )(xn2, wq, sk)


def _gelu_tanh(x):
    return 0.5 * x * (1.0 + jnp.tanh(math.sqrt(2.0 / math.pi) * (x + 0.044715 * (x * x * x))))


def _peer_body(x_ref, u_ref, vt_ref, r2_ref, e2_ref, jn_ref, e1_ref, h1_ref, o_ref, acc_ref, *,
               heads, nk, nb, cw):
    j = pl.program_id(1)
    x = x_ref[...]
    nchunk = nb * nk // cw

    def scores(c):
        return lax.dot_general(u_ref[c * cw:(c + 1) * cw, :], x, _NT, preferred_element_type=F32)

    a_next = scores(0)
    hts = []
    for c in range(nchunk):
        a = a_next
        if c + 1 < nchunk:
            a_next = scores(c + 1)
        act = _gelu_tanh(a).astype(BF16)
        ws = []
        for nl in range(cw // nk):
            n1 = c * (cw // nk) + nl
            w = None
            for h in range(heads):
                jn = jn_ref[h, n1:n1 + 1, :].astype(BF16)
                e1 = e1_ref[h, n1:n1 + 1, :].astype(BF16)
                t = jnp.where(r2_ref[h] < jn, e2_ref[h] * e1, jnp.zeros((), BF16))
                w = t if w is None else w + t
            ws.append(w)
        hts.append(jnp.concatenate(ws, axis=0) * act)
    total = jnp.dot(vt_ref[...], jnp.concatenate(hts, axis=0), preferred_element_type=F32)

    @pl.when(j == 0)
    def _():
        acc_ref[...] = total

    @pl.when(j > 0)
    def _():
        acc_ref[...] += total

    @pl.when(j == pl.num_programs(1) - 1)
    def _():
        o_ref[...] = h1_ref[...] + acc_ref[...].T


def _peer_experts(xn2, u_bf, vt_bf, r2, e2, jn, e1, h1, *, heads, nk, nb, tm):
    T, D = xn2.shape
    E = u_bf.shape[0]
    eb = nb * nk
    body = functools.partial(_peer_body, heads=heads, nk=nk, nb=nb, cw=MXU_DIM)
    key_spec = pl.BlockSpec((heads, nk, tm), lambda i, j: (0, 0, i))
    row_spec = pl.BlockSpec((heads, nb, tm), lambda i, j: (0, j, i))
    return pl.pallas_call(
        body,
        grid=(T // tm, E // eb),
        in_specs=[pl.BlockSpec((tm, D), lambda i, j: (i, 0)),
                  pl.BlockSpec((eb, D), lambda i, j: (j, 0)),
                  pl.BlockSpec((D, eb), lambda i, j: (0, j)),
                  key_spec, key_spec, row_spec, row_spec,
                  pl.BlockSpec((tm, D), lambda i, j: (i, 0))],
        out_specs=pl.BlockSpec((tm, D), lambda i, j: (i, 0)),
        out_shape=jax.ShapeDtypeStruct((T, D), F32),
        scratch_shapes=[pltpu.VMEM((D, tm), F32)],
        compiler_params=_cparams(("parallel", "arbitrary")),
        name="peer_experts",
    )(xn2, u_bf, vt_bf, r2, e2, jn, e1, h1)


def kernel(x, meta_tokens, norm1_g, w_in, da_q_norm_g, da_k_norm_g, lambda_q1, lambda_k1,
           lambda_q2, lambda_k2, da_out_norm_g, na_q_norm_g, na_k_norm_g, na_rpb,
           w_branch_da, w_branch_na, w_out, norm2_g, peer_w_query, peer_sub_keys, peer_u, peer_v):
    B, S, D = x.shape
    n_meta = meta_tokens.shape[0]
    da_hd = da_q_norm_g.shape[1]
    da_vw = w_branch_da.shape[1]
    da_heads = da_vw // (2 * da_hd)
    da_qkw = da_heads * 2 * da_hd
    na_hd = na_q_norm_g.shape[1]
    na_w = w_branch_na.shape[1]
    na_heads = na_rpb.shape[1]
    nk = peer_sub_keys.shape[2]
    peer_heads = peer_w_query.shape[2] // (2 * nk)
    splits = (da_qkw, da_qkw, da_vw, na_w, na_w, na_w, D, D)
    col = np.concatenate([[0], np.cumsum(splits)]).tolist()
    assert w_in.shape[0] == 1
    assert w_in.shape[2] == col[-1] and S % ROW_BLOCK == 0 and S % GRID_W == 0
    assert da_hd == na_hd and 2 * da_hd == LANES and nk == LANES and n_meta <= META_ROWS
    l = 0

    T = B * S
    tn = 512
    tm_proj = max(t for t in (1024, 512, 256) if T % t == 0)
    tm_peer = 512 if T % 512 == 0 else 256

    xr = x.reshape(T, D)
    xm = jnp.pad(meta_tokens.astype(x.dtype), ((0, META_ROWS - n_meta), (0, 0)))
    log2e = math.log2(math.e)
    slopes = jnp.asarray(2.0 ** (-8.0 * np.arange(1, da_heads + 1) / da_heads) * log2e, dtype=F32)
    ones = lambda n: jnp.ones((n,), F32)

    gcols = jnp.concatenate([
        jnp.tile(da_q_norm_g[l].astype(F32), da_qkw // da_hd) * (da_hd ** -0.5 * log2e),
        jnp.tile(da_k_norm_g[l].astype(F32), da_qkw // da_hd),
        ones(da_vw),
        jnp.tile(na_q_norm_g[l].astype(F32), na_heads) * na_hd ** -0.5,
        jnp.tile(na_k_norm_g[l].astype(F32), na_heads),
        ones(na_w + 2 * D)])[None]
    norm_ranges = ((col[0] // tn, col[2] // tn), (col[3] // tn, col[5] // tn))
    proj_args = dict(norm_ranges=norm_ranges, sig_start=col[6] // tn, head_dim=da_hd, tn=tn)
    g1 = norm1_g[l][None].astype(F32)
    w_in_bf = w_in[l].astype(BF16)
    proj = _inproj(xr, g1, w_in_bf, gcols, tm=tm_proj, **proj_args)
    projm = _inproj(xm, g1, w_in_bf, gcols, tm=META_ROWS, **proj_args)

    lam_init = 0.8 - 0.6 * math.exp(-0.3 * l)
    lamv = jnp.stack([lambda_q1[l], lambda_k1[l], lambda_q2[l], lambda_k2[l]]).astype(F32)
    o_da = _diff_attention(proj, projm, slopes, lamv, da_out_norm_g[l][None].astype(F32),
                           B=B, S=S, n_meta=n_meta, H=da_heads, hd=da_hd,
                           q_col=col[0], k_col=col[1], v_col=col[2], lam_init=lam_init,
                           tq=ROW_BLOCK, nch=8)

    qrows = ROW_BLOCK // GRID_W
    bias_tab = _na_bias_table(na_rpb[l], S // GRID_W, qrows)
    o_na = _nbhd_attention(proj, projm, bias_tab, B=B, S=S, n_meta=n_meta, H=na_heads, hd=na_hd,
                           q_col=col[3], k_col=col[4], v_col=col[5], qrows=qrows, nh=4)

    h1, xn2 = _mix(o_da, o_na, proj, xr, w_branch_da[l].astype(BF16),
                   w_branch_na[l].astype(BF16), w_out[l].astype(BF16),
                   norm2_g[l][None].astype(F32), gda_col=col[6], gna_col=col[7], tm=ROW_BLOCK)

    r2, e2, jn, e1 = _peer_route(xn2, peer_w_query[l].astype(BF16),
                                 peer_sub_keys[l].astype(BF16), heads=peer_heads, nk=nk,
                                 topk=PEER_TOPK, tm=ROW_BLOCK)
    out = _peer_experts(xn2, peer_u[l].astype(BF16), peer_v[l].T.astype(BF16), r2, e2, jn, e1, h1,
                        heads=peer_heads, nk=nk, nb=8, tm=tm_peer)
    return out.reshape(B, S, D)
```
